```python
import jax, jax.numpy as jnp
from jax import lax
import numpy as np

D_MODEL = 1024
BATCH = 16
SEQ = 2048
DEPTH = 2

N_MIXERS = 2
GRID_W = 64
RMS_EPS = 1e-6
GN_EPS = 1e-6
ROPE_THETA = 10000.0

RET_HEADS = 4
RET_QK_DIM = D_MODEL // RET_HEADS
RET_V_DIM = 2 * RET_QK_DIM
RET_CHUNK = 128
RET_QK_W = RET_HEADS * RET_QK_DIM
RET_V_W = RET_HEADS * RET_V_DIM
RET_IN_W = 2 * RET_QK_W + 2 * RET_V_W

ATTN_HEAD_DIM = 128
ATTN_Q_HEADS = D_MODEL // ATTN_HEAD_DIM
ATTN_KV_HEADS = 2
ATTN_GROUP = ATTN_Q_HEADS // ATTN_KV_HEADS
ATTN_IN_W = (ATTN_Q_HEADS + 2 * ATTN_KV_HEADS) * ATTN_HEAD_DIM
Q_BLOCK = 128

D_FF = 4 * D_MODEL

N_RET_LAYERS = (DEPTH + 1) // 2
N_ATTN_LAYERS = DEPTH // 2

kernel_name = "hybrid_retention_gqa_axial_encoder"


def rms_norm(x, g):
    xf = x.astype(jnp.float32)
    y = xf * lax.rsqrt(jnp.mean(xf * xf, axis=-1, keepdims=True) + RMS_EPS)
    return (y * g.astype(jnp.float32)).astype(x.dtype)


def axial_rope_tables(seq, rot_dim):
    rows = seq // GRID_W
    row = jnp.broadcast_to(jnp.arange(rows, dtype=jnp.float32)[:, None], (rows, GRID_W)).reshape(seq)
    col = jnp.broadcast_to(jnp.arange(GRID_W, dtype=jnp.float32)[None, :], (rows, GRID_W)).reshape(seq)
    per_axis = rot_dim // 2
    n_freq = per_axis // 2
    inv_freq = ROPE_THETA ** (-jnp.arange(n_freq, dtype=jnp.float32) * 2.0 / per_axis)
    ang = jnp.concatenate([row[:, None] * inv_freq[None, :], col[:, None] * inv_freq[None, :]], axis=-1)
    return jnp.cos(ang), jnp.sin(ang)


def apply_rope(x, cos, sin):
    xf = x.astype(jnp.float32)
    half = xf.shape[-1] // 2
    x1, x2 = xf[..., :half], xf[..., half:]
    out = jnp.concatenate([x1 * cos - x2 * sin, x1 * sin + x2 * cos], axis=-1)
    return out.astype(x.dtype)


def chunkwise_retention(q, k, v, log_gamma, include_diag):
    B, H, S, dk = q.shape
    dv = v.shape[-1]
    C = RET_CHUNK
    N = S // C
    idx = jnp.arange(C, dtype=jnp.float32)
    lg = log_gamma.astype(jnp.float32)
    diff = idx[:, None] - idx[None, :]
    mask = (diff >= 0) if include_diag else (diff > 0)
    decay_intra = jnp.where(mask[None], jnp.exp(lg[:, None, None] * jnp.where(mask, diff, 0.0)[None]), 0.0)
    xi = jnp.exp(lg[:, None] * (idx[None, :] + 1.0))
    zeta = jnp.exp(lg[:, None] * (C - 1.0 - idx[None, :]))
    gamma_c = jnp.exp(lg * C)

    def to_chunks(t):
        return t.reshape(B, H, N, C, t.shape[-1]).transpose(2, 0, 1, 3, 4)

    def step(state, xs):
        qb, kb, vb = xs
        scores = jnp.einsum('bhid,bhjd->bhij', qb, kb) * decay_intra[None]
        intra = jnp.einsum('bhij,bhjv->bhiv', scores, vb)
        inter = jnp.einsum('bhid,bhdv->bhiv', qb, state) * xi[None, :, :, None]
        new_state = state * gamma_c[None, :, None, None] + jnp.einsum(
            'bhjd,bhjv->bhdv', kb * zeta[None, :, :, None], vb)
        return new_state, intra + inter

    state0 = jnp.zeros((B, H, dk, dv), jnp.float32)
    _, ys = lax.scan(step, state0, (to_chunks(q), to_chunks(k), to_chunks(v)))
    return ys.transpose(1, 2, 0, 3, 4).reshape(B, H, S, dv)


def retention_mixer(h, w_in, w_out, decay_fwd, decay_bwd, cos, sin):
    B, S, _ = h.shape
    proj = h @ w_in
    q, k, v, g = jnp.split(proj, [RET_QK_W, 2 * RET_QK_W, 2 * RET_QK_W + RET_V_W], axis=-1)
    q = q.reshape(B, S, RET_HEADS, RET_QK_DIM).transpose(0, 2, 1, 3)
    k = k.reshape(B, S, RET_HEADS, RET_QK_DIM).transpose(0, 2, 1, 3)
    v = v.reshape(B, S, RET_HEADS, RET_V_DIM).transpose(0, 2, 1, 3).astype(jnp.float32)
    q = apply_rope(q, cos, sin).astype(jnp.float32)
    k = apply_rope(k, cos, sin).astype(jnp.float32) * (RET_QK_DIM ** -0.5)
    log_gf = -jnp.exp(decay_fwd.astype(jnp.float32))
    log_gb = -jnp.exp(decay_bwd.astype(jnp.float32))
    y_fwd = chunkwise_retention(q, k, v, log_gf, include_diag=True)
    y_bwd = jnp.flip(chunkwise_retention(jnp.flip(q, 2), jnp.flip(k, 2), jnp.flip(v, 2),
                                         log_gb, include_diag=False), 2)
    y = y_fwd + y_bwd
    mu = jnp.mean(y, axis=-1, keepdims=True)
    var = jnp.mean(jnp.square(y - mu), axis=-1, keepdims=True)
    y = (y - mu) * lax.rsqrt(var + GN_EPS)
    y = y.transpose(0, 2, 1, 3).reshape(B, S, RET_V_W).astype(h.dtype)
    return (jax.nn.silu(g) * y) @ w_out


def attention_mixer(h, w_in, w_out, q_gain, k_gain, cos, sin):
    B, S, _ = h.shape
    proj = h @ w_in
    q_w = ATTN_Q_HEADS * ATTN_HEAD_DIM
    kv_w = ATTN_KV_HEADS * ATTN_HEAD_DIM
    q, k, v = jnp.split(proj, [q_w, q_w + kv_w], axis=-1)
    q = q.reshape(B, S, ATTN_KV_HEADS, ATTN_GROUP, ATTN_HEAD_DIM).transpose(0, 2, 3, 1, 4)
    k = k.reshape(B, S, ATTN_KV_HEADS, ATTN_HEAD_DIM).transpose(0, 2, 1, 3)
    v = v.reshape(B, S, ATTN_KV_HEADS, ATTN_HEAD_DIM).transpose(0, 2, 1, 3)
    q = apply_rope(rms_norm(q, q_gain), cos, sin) * (ATTN_HEAD_DIM ** -0.5)
    k = apply_rope(rms_norm(k, k_gain), cos, sin)
    nb = S // Q_BLOCK
    qb = q.reshape(B, ATTN_KV_HEADS, ATTN_GROUP, nb, Q_BLOCK, ATTN_HEAD_DIM).transpose(3, 0, 1, 2, 4, 5)

    def one_block(q_blk):
        s = jnp.einsum('bkgqd,bksd->bkgqs', q_blk, k).astype(jnp.float32)
        p = jax.nn.softmax(s, axis=-1).astype(v.dtype)
        return jnp.einsum('bkgqs,bksd->bkgqd', p, v)

    o = lax.map(one_block, qb)
    o = o.transpose(1, 0, 4, 2, 3, 5).reshape(B, S, D_MODEL)
    return o @ w_out


def sq_relu_mlp(h, w1, w2):
    return jnp.square(jax.nn.relu(h @ w1)) @ w2


def setup_inputs(seed: int = 0) -> dict:
    key = jax.random.key(seed)
    ks = jax.random.split(key, 16)
    f32 = jnp.float32

    def w(k, shape, fan_in):
        return jax.random.normal(k, shape, f32) * (fan_in ** -0.5)

    def gain(k, shape):
        return 1.0 + 0.05 * jax.random.normal(k, shape, f32)

    base = jnp.log(-jnp.log(1.0 - 2.0 ** (-5.0 - jnp.arange(RET_HEADS, dtype=f32))))
    return {
        "x": jax.random.normal(ks[0], (BATCH, SEQ, D_MODEL), f32),
        "norm_mix": gain(ks[1], (DEPTH, D_MODEL)),
        "norm_mlp": gain(ks[2], (DEPTH, D_MODEL)),
        "mlp_w1": w(ks[3], (DEPTH, D_MODEL, D_FF), D_MODEL),
        "mlp_w2": w(ks[4], (DEPTH, D_FF, D_MODEL), D_FF),
        "ret_w_in": w(ks[5], (N_RET_LAYERS, D_MODEL, RET_IN_W), D_MODEL),
        "ret_w_out": w(ks[6], (N_RET_LAYERS, RET_V_W, D_MODEL), RET_V_W),
        "ret_decay_fwd": base[None, :] + 0.05 * jax.random.normal(ks[7], (N_RET_LAYERS, RET_HEADS), f32),
        "ret_decay_bwd": base[None, :] + 0.05 * jax.random.normal(ks[8], (N_RET_LAYERS, RET_HEADS), f32),
        "attn_w_in": w(ks[9], (N_ATTN_LAYERS, D_MODEL, ATTN_IN_W), D_MODEL),
        "attn_w_out": w(ks[10], (N_ATTN_LAYERS, D_MODEL, D_MODEL), D_MODEL),
        "attn_q_norm": gain(ks[11], (N_ATTN_LAYERS, ATTN_HEAD_DIM)),
        "attn_k_norm": gain(ks[12], (N_ATTN_LAYERS, ATTN_HEAD_DIM)),
        "final_norm": gain(ks[13], (D_MODEL,)),
    }


def reference(x, norm_mix, norm_mlp, mlp_w1, mlp_w2, ret_w_in, ret_w_out, ret_decay_fwd,
              ret_decay_bwd, attn_w_in, attn_w_out, attn_q_norm, attn_k_norm, final_norm):
    S = x.shape[1]
    cos_r, sin_r = axial_rope_tables(S, RET_QK_DIM)
    cos_a, sin_a = axial_rope_tables(S, ATTN_HEAD_DIM)
    h = x
    for i in range(DEPTH):
        u = rms_norm(h, norm_mix[i])
        j = i // N_MIXERS
        if i % N_MIXERS == 0:
            h = h + retention_mixer(u, ret_w_in[j], ret_w_out[j], ret_decay_fwd[j],
                                    ret_decay_bwd[j], cos_r, sin_r)
        else:
            h = h + attention_mixer(u, attn_w_in[j], attn_w_out[j], attn_q_norm[j],
                                    attn_k_norm[j], cos_a, sin_a)
        u = rms_norm(h, norm_mlp[i])
        h = h + sq_relu_mlp(u, mlp_w1[i], mlp_w2[i])
    return rms_norm(h, final_norm)
```

```python
import functools
import math

import jax
import jax.numpy as jnp
from jax import lax
from jax.experimental import pallas as pl
from jax.experimental.pallas import tpu as pltpu

BF16 = jnp.bfloat16
F32 = jnp.float32

GRID_W = 64
RMS_EPS = 1e-6
GN_EPS = 1e-6
ROPE_THETA = 10000.0

RET_HEADS = 4
RET_QK_DIM = 256
RET_V_DIM = 512
RET_QK_W = RET_HEADS * RET_QK_DIM
RET_V_W = RET_HEADS * RET_V_DIM
RET_CHUNK = 256

ATTN_HEAD_DIM = 128
ATTN_Q_HEADS = 8
ATTN_KV_HEADS = 2
ATTN_GROUP = ATTN_Q_HEADS // ATTN_KV_HEADS
ATTN_Q_W = ATTN_Q_HEADS * ATTN_HEAD_DIM
ATTN_KV_W = ATTN_KV_HEADS * ATTN_HEAD_DIM
ATTN_IN_W = ATTN_Q_W + 2 * ATTN_KV_W

TOKEN_TILE = 512
ATTN_Q_TILE = 256
FF_CHUNK = 1024
V7X_VMEM_LIMIT_BYTES = 56 * 1024 * 1024


def _dot(a, b):
    return jnp.dot(a, b, preferred_element_type=F32)


def _dot_nt(a, b):
    return lax.dot_general(a, b, (((1,), (1,)), ((), ())), preferred_element_type=F32)


def _rms_norm_rows(x, gain):
    ms = jnp.mean(x * x, axis=-1, keepdims=True)
    return x * lax.rsqrt(ms + RMS_EPS) * gain


def _resident(shape):
    nd = len(shape)
    return pl.BlockSpec(shape, lambda *_: (0,) * nd, pipeline_mode=pl.Buffered(1))


def _axial_rope_tables(seq, rot_dim):
    rows = seq // GRID_W
    row = jnp.broadcast_to(jnp.arange(rows, dtype=F32)[:, None], (rows, GRID_W)).reshape(seq)
    col = jnp.broadcast_to(jnp.arange(GRID_W, dtype=F32)[None, :], (rows, GRID_W)).reshape(seq)
    per_axis = rot_dim // 2
    n_freq = per_axis // 2
    inv_freq = ROPE_THETA ** (-jnp.arange(n_freq, dtype=F32) * 2.0 / per_axis)
    ang = jnp.concatenate([row[:, None] * inv_freq[None, :], col[:, None] * inv_freq[None, :]], axis=-1)
    return jnp.cos(ang), jnp.sin(ang)


def _ret_proj_kernel(h_ref, gain_ref, wq_ref, wkt_ref, wvg_ref, cos_ref, sin_ref, cost_ref, sint_ref,
                     q_ref, kt_ref, v_ref, gs_ref):
    u = _rms_norm_rows(h_ref[...], gain_ref[...]).astype(BF16)
    half = RET_QK_DIM // 2
    cos = cos_ref[...]
    sin = sin_ref[...]
    for hh in range(RET_HEADS):
        lo = hh * RET_QK_DIM
        qh = _dot(u, wq_ref[:, lo:lo + RET_QK_DIM])
        x1 = qh[:, :half]
        x2 = qh[:, half:]
        q_ref[:, lo:lo + half] = (x1 * cos - x2 * sin).astype(BF16)
        q_ref[:, lo + half:lo + RET_QK_DIM] = (x1 * sin + x2 * cos).astype(BF16)
    cos_t = cost_ref[...]
    sin_t = sint_ref[...]
    k_scale = RET_QK_DIM ** -0.5
    for hh in range(RET_HEADS):
        lo = hh * RET_QK_DIM
        kh = _dot_nt(wkt_ref[lo:lo + RET_QK_DIM, :], u)
        x1 = kh[:half]
        x2 = kh[half:]
        kt_ref[0, lo:lo + half, :] = ((x1 * cos_t - x2 * sin_t) * k_scale).astype(BF16)
        kt_ref[0, lo + half:lo + RET_QK_DIM, :] = ((x1 * sin_t + x2 * cos_t) * k_scale).astype(BF16)
    for c in range(RET_HEADS):
        lo = c * RET_V_DIM
        v_ref[:, lo:lo + RET_V_DIM] = _dot(u, wvg_ref[:, lo:lo + RET_V_DIM]).astype(BF16)
    for c in range(RET_HEADS):
        lo = c * RET_V_DIM
        g = _dot(u, wvg_ref[:, RET_V_W + lo:RET_V_W + lo + RET_V_DIM])
        gs_ref[:, lo:lo + RET_V_DIM] = (g * jax.nn.sigmoid(g)).astype(BF16)


def _ret_proj(h, gain, wq, wkt, wvg, cos, sin, batch, seq):
    tokens, d = h.shape
    tm = TOKEN_TILE
    tiles_per_seq = seq // tm
    half = RET_QK_DIM // 2
    cos_t = cos.T
    sin_t = sin.T
    row_tile = lambda width: pl.BlockSpec((tm, width), lambda i: (i, 0))
    return pl.pallas_call(
        _ret_proj_kernel,
        grid=(tokens // tm,),
        in_specs=[
            row_tile(d),
            _resident((1, d)),
            _resident(wq.shape),
            _resident(wkt.shape),
            _resident(wvg.shape),
            pl.BlockSpec((tm, half), lambda i: (i % tiles_per_seq, 0)),
            pl.BlockSpec((tm, half), lambda i: (i % tiles_per_seq, 0)),
            pl.BlockSpec((half, tm), lambda i: (0, i % tiles_per_seq)),
            pl.BlockSpec((half, tm), lambda i: (0, i % tiles_per_seq)),
        ],
        out_specs=[
            row_tile(RET_QK_W),
            pl.BlockSpec((1, RET_QK_W, tm), lambda i: (i // tiles_per_seq, 0, i % tiles_per_seq)),
            row_tile(RET_V_W),
            row_tile(RET_V_W),
        ],
        out_shape=[
            jax.ShapeDtypeStruct((tokens, RET_QK_W), BF16),
            jax.ShapeDtypeStruct((batch, RET_QK_W, seq), BF16),
            jax.ShapeDtypeStruct((tokens, RET_V_W), BF16),
            jax.ShapeDtypeStruct((tokens, RET_V_W), BF16),
        ],
        compiler_params=pltpu.CompilerParams(
            dimension_semantics=("arbitrary",), vmem_limit_bytes=V7X_VMEM_LIMIT_BYTES),
        name="ret_proj",
    )(h, gain, wq, wkt, wvg, cos, sin, cos_t, sin_t)


def _retention_kernel(dec_f_ref, dec_b_ref, q_ref, kt_ref, v_ref, gs_ref, y_ref, yacc_ref, state_ref):
    head = pl.program_id(1)
    c = RET_CHUNK
    n_chunks = q_ref.shape[1] // c
    dk = RET_QK_DIM
    lgf = -jnp.exp(jnp.full((c, c), dec_f_ref[head], F32))
    lgb = -jnp.exp(jnp.full((c, c), dec_b_ref[head], F32))
    row = lax.broadcasted_iota(jnp.int32, (c, c), 0).astype(F32)
    col = lax.broadcasted_iota(jnp.int32, (c, c), 1).astype(F32)
    diff = row - col
    decay = jnp.where(diff >= 0, jnp.exp(lgf * jnp.maximum(diff, 0.0)),
                      jnp.exp(lgb * jnp.maximum(-diff, 0.0)))
    xi_f = jnp.exp(lgf * (row + 1.0))
    xi_b = jnp.exp(lgb * (float(c) - row))
    zeta_f = jnp.exp(lgf * (float(c) - 1.0 - col))
    zeta_b = jnp.exp(lgb * col)
    state_shape = state_ref.shape
    gc_f = jnp.exp(-jnp.exp(jnp.full(state_shape, dec_f_ref[head], F32)) * float(c))
    gc_b = jnp.exp(-jnp.exp(jnp.full(state_shape, dec_b_ref[head], F32)) * float(c))

    def chunk(i):
        rows = slice(i * c, (i + 1) * c)
        return q_ref[0, rows, :], kt_ref[0, :, rows], v_ref[0, rows, :]

    state_ref[...] = jnp.zeros_like(state_ref)
    for i in range(n_chunks):
        qi, kti, vi = chunk(i)
        scores = _dot(qi, kti) * decay
        y = _dot(scores.astype(BF16), vi)
        if i > 0:
            qs = (qi.astype(F32) * xi_f).astype(BF16)
            y = y + _dot(qs, state_ref[...].astype(BF16))
        yacc_ref[i * c:(i + 1) * c, :] = y
        if i < n_chunks - 1:
            ks = (kti.astype(F32) * zeta_f).astype(BF16)
            state_ref[...] = state_ref[...] * gc_f + _dot(ks, vi)

    state_ref[...] = jnp.zeros_like(state_ref)
    for i in range(n_chunks - 1, -1, -1):
        qi, kti, vi = chunk(i)
        y = yacc_ref[i * c:(i + 1) * c, :]
        if i < n_chunks - 1:
            qs = (qi.astype(F32) * xi_b).astype(BF16)
            y = y + _dot(qs, state_ref[...].astype(BF16))
        mu = jnp.mean(y, axis=-1, keepdims=True)
        yc = y - mu
        var = jnp.mean(yc * yc, axis=-1, keepdims=True)
        yn = yc * lax.rsqrt(var + GN_EPS)
        y_ref[0, i * c:(i + 1) * c, :] = (gs_ref[0, i * c:(i + 1) * c, :].astype(F32) * yn).astype(BF16)
        if i > 0:
            ks = (kti.astype(F32) * zeta_b).astype(BF16)
            state_ref[...] = state_ref[...] * gc_b + _dot(ks, vi)


def _retention(q, kt, v, gs, dec_f, dec_b):
    batch, seq, _ = q.shape
    assert RET_CHUNK == RET_QK_DIM and seq % RET_CHUNK == 0
    grid_spec = pltpu.PrefetchScalarGridSpec(
        num_scalar_prefetch=2,
        grid=(batch, RET_HEADS),
        in_specs=[
            pl.BlockSpec((1, seq, RET_QK_DIM), lambda b, h, *_: (b, 0, h)),
            pl.BlockSpec((1, RET_QK_DIM, seq), lambda b, h, *_: (b, h, 0)),
            pl.BlockSpec((1, seq, RET_V_DIM), lambda b, h, *_: (b, 0, h)),
            pl.BlockSpec((1, seq, RET_V_DIM), lambda b, h, *_: (b, 0, h)),
        ],
        out_specs=pl.BlockSpec((1, seq, RET_V_DIM), lambda b, h, *_: (b, 0, h)),
        scratch_shapes=[pltpu.VMEM((seq, RET_V_DIM), F32), pltpu.VMEM((RET_QK_DIM, RET_V_DIM), F32)],
    )
    return pl.pallas_call(
        _retention_kernel,
        grid_spec=grid_spec,
        out_shape=jax.ShapeDtypeStruct((batch, seq, RET_V_W), BF16),
        compiler_params=pltpu.CompilerParams(
            dimension_semantics=("arbitrary", "arbitrary"), vmem_limit_bytes=V7X_VMEM_LIMIT_BYTES),
        name="retention",
    )(dec_f, dec_b, q, kt, v, gs)


def _mix_out_mlp(y_ref, h_ref, wo_ref, gmlp_ref, w1_ref, w2_ref):
    h1 = h_ref[...] + _dot(y_ref[...], wo_ref[...])
    u = _rms_norm_rows(h1, gmlp_ref[...]).astype(BF16)
    h2 = h1
    d_ff = w1_ref.shape[1]
    for c in range(d_ff // FF_CHUNK):
        lo = c * FF_CHUNK
        a = jnp.maximum(_dot(u, w1_ref[:, lo:lo + FF_CHUNK]), 0.0)
        h2 = h2 + _dot((a * a).astype(BF16), w2_ref[lo:lo + FF_CHUNK, :])
    return h2


def _tail_attn_proj_kernel(y_ref, h_ref, wo_ref, gmlp_ref, w1_ref, w2_ref,
                           gmix_ref, win_ref, qn_ref, kn_ref, cos2_ref, sin2_ref,
                           hout_ref, qkv_ref):
    h2 = _mix_out_mlp(y_ref, h_ref, wo_ref, gmlp_ref, w1_ref, w2_ref)
    hout_ref[...] = h2
    u = _rms_norm_rows(h2, gmix_ref[...]).astype(BF16)
    cos2 = cos2_ref[...]
    sin2 = sin2_ref[...]
    d = ATTN_HEAD_DIM
    q_scale = ATTN_HEAD_DIM ** -0.5

    def norm_rope(x, gain):
        xn = _rms_norm_rows(x, gain)
        return xn * cos2 + pltpu.roll(xn, d // 2, 1) * sin2

    for hh in range(ATTN_Q_HEADS):
        lo = hh * d
        qh = _dot(u, win_ref[:, lo:lo + d])
        qkv_ref[:, lo:lo + d] = (norm_rope(qh, qn_ref[...]) * q_scale).astype(BF16)
    for hh in range(ATTN_KV_HEADS):
        lo = ATTN_Q_W + hh * d
        kh = _dot(u, win_ref[:, lo:lo + d])
        qkv_ref[:, lo:lo + d] = norm_rope(kh, kn_ref[...]).astype(BF16)
    lo = ATTN_Q_W + ATTN_KV_W
    qkv_ref[:, lo:lo + ATTN_KV_W] = _dot(u, win_ref[:, lo:lo + ATTN_KV_W]).astype(BF16)


def _tail_final_kernel(y_ref, h_ref, wo_ref, gmlp_ref, w1_ref, w2_ref, gfin_ref, out_ref):
    h2 = _mix_out_mlp(y_ref, h_ref, wo_ref, gmlp_ref, w1_ref, w2_ref)
    out_ref[...] = _rms_norm_rows(h2, gfin_ref[...])


def _tail_common_specs(y, h, wo, w1, w2):
    tm = TOKEN_TILE
    d = h.shape[1]
    return [
        pl.BlockSpec((tm, y.shape[1]), lambda i: (i, 0)),
        pl.BlockSpec((tm, d), lambda i: (i, 0)),
        _resident(wo.shape),
        _resident((1, d)),
        _resident(w1.shape),
        _resident(w2.shape),
    ]


def _tail_attn_proj(y, h, wo, gmlp, w1, w2, gmix, win, qn, kn, cos2, sin2, seq):
    tokens, d = h.shape
    tm = TOKEN_TILE
    tiles_per_seq = seq // tm
    pos_tile = pl.BlockSpec((tm, ATTN_HEAD_DIM), lambda i: (i % tiles_per_seq, 0))
    return pl.pallas_call(
        _tail_attn_proj_kernel,
        grid=(tokens // tm,),
        in_specs=_tail_common_specs(y, h, wo, w1, w2) + [
            _resident((1, d)),
            _resident(win.shape),
            _resident((1, ATTN_HEAD_DIM)),
            _resident((1, ATTN_HEAD_DIM)),
            pos_tile,
            pos_tile,
        ],
        out_specs=[
            pl.BlockSpec((tm, d), lambda i: (i, 0)),
            pl.BlockSpec((tm, ATTN_IN_W), lambda i: (i, 0)),
        ],
        out_shape=[
            jax.ShapeDtypeStruct((tokens, d), F32),
            jax.ShapeDtypeStruct((tokens, ATTN_IN_W), BF16),
        ],
        compiler_params=pltpu.CompilerParams(
            dimension_semantics=("arbitrary",), vmem_limit_bytes=V7X_VMEM_LIMIT_BYTES),
        name="tail_attn_proj",
    )(y, h, wo, gmlp, w1, w2, gmix, win, qn, kn, cos2, sin2)


def _tail_final(y, h, wo, gmlp, w1, w2, gfin):
    tokens, d = h.shape
    tm = TOKEN_TILE
    return pl.pallas_call(
        _tail_final_kernel,
        grid=(tokens // tm,),
        in_specs=_tail_common_specs(y, h, wo, w1, w2) + [_resident((1, d))],
        out_specs=pl.BlockSpec((tm, d), lambda i: (i, 0)),
        out_shape=jax.ShapeDtypeStruct((tokens, d), F32),
        compiler_params=pltpu.CompilerParams(
            dimension_semantics=("arbitrary",), vmem_limit_bytes=V7X_VMEM_LIMIT_BYTES),
        name="tail_final",
    )(y, h, wo, gmlp, w1, w2, gfin)


def _attention_kernel(q_ref, k_ref, v_ref, o_ref):
    d = ATTN_HEAD_DIM
    k = k_ref[0]
    v = v_ref[0]
    for g in range(ATTN_GROUP):
        q = q_ref[0, :, g * d:(g + 1) * d]
        s = _dot_nt(q, k)
        m = jnp.max(s, axis=-1, keepdims=True)
        p = jnp.exp(s - m)
        l = jnp.sum(p, axis=-1, keepdims=True)
        o = _dot(p.astype(BF16), v) / l
        o_ref[0, :, g * d:(g + 1) * d] = o.astype(BF16)


def _attention(qkv):
    batch, seq, _ = qkv.shape
    tq = ATTN_Q_TILE
    group_w = ATTN_GROUP * ATTN_HEAD_DIM
    k_block0 = ATTN_Q_W // ATTN_HEAD_DIM
    v_block0 = (ATTN_Q_W + ATTN_KV_W) // ATTN_HEAD_DIM
    return pl.pallas_call(
        _attention_kernel,
        grid=(batch, ATTN_KV_HEADS, seq // tq),
        in_specs=[
            pl.BlockSpec((1, tq, group_w), lambda b, kk, i: (b, i, kk)),
            pl.BlockSpec((1, seq, ATTN_HEAD_DIM), lambda b, kk, i: (b, 0, k_block0 + kk)),
            pl.BlockSpec((1, seq, ATTN_HEAD_DIM), lambda b, kk, i: (b, 0, v_block0 + kk)),
        ],
        out_specs=pl.BlockSpec((1, tq, group_w), lambda b, kk, i: (b, i, kk)),
        out_shape=jax.ShapeDtypeStruct((batch, seq, ATTN_Q_W), BF16),
        compiler_params=pltpu.CompilerParams(
            dimension_semantics=("arbitrary", "arbitrary", "arbitrary"),
            vmem_limit_bytes=V7X_VMEM_LIMIT_BYTES),
        name="attention",
    )(qkv, qkv, qkv)


def kernel(x, norm_mix, norm_mlp, mlp_w1, mlp_w2, ret_w_in, ret_w_out, ret_decay_fwd, ret_decay_bwd,
           attn_w_in, attn_w_out, attn_q_norm, attn_k_norm, final_norm):
    batch, seq, d = x.shape
    assert norm_mix.shape[0] == 2 and ret_w_in.shape[0] == 1 and attn_w_in.shape[0] == 1
    assert seq % TOKEN_TILE == 0 and seq % ATTN_Q_TILE == 0 and seq % GRID_W == 0
    tokens = batch * seq
    h = x.reshape(tokens, d)

    cos_r, sin_r = _axial_rope_tables(seq, RET_QK_DIM)
    cos_a, sin_a = _axial_rope_tables(seq, ATTN_HEAD_DIM)
    cos2 = jnp.concatenate([cos_a, cos_a], axis=-1)
    sin2 = jnp.concatenate([-sin_a, sin_a], axis=-1)

    w_in = ret_w_in[0]
    wq = w_in[:, :RET_QK_W].astype(BF16)
    wkt = w_in[:, RET_QK_W:2 * RET_QK_W].T.astype(BF16)
    wvg = w_in[:, 2 * RET_QK_W:].astype(BF16)
    q, kt, v, gs = _ret_proj(h, norm_mix[0][None, :], wq, wkt, wvg, cos_r, sin_r, batch, seq)
    y = _retention(q.reshape(batch, seq, RET_QK_W), kt, v.reshape(batch, seq, RET_V_W),
                   gs.reshape(batch, seq, RET_V_W), ret_decay_fwd[0], ret_decay_bwd[0])
    h, qkv = _tail_attn_proj(
        y.reshape(tokens, RET_V_W), h, ret_w_out[0].astype(BF16), norm_mlp[0][None, :],
        mlp_w1[0].astype(BF16), mlp_w2[0].astype(BF16), norm_mix[1][None, :],
        attn_w_in[0].astype(BF16), attn_q_norm[0][None, :], attn_k_norm[0][None, :], cos2, sin2, seq)

    o = _attention(qkv.reshape(batch, seq, ATTN_IN_W))
    out = _tail_final(o.reshape(tokens, d), h, attn_w_out[0].astype(BF16), norm_mlp[1][None, :],
                      mlp_w1[1].astype(BF16), mlp_w2[1].astype(BF16), final_norm[None, :])
    return out.reshape(batch, seq, d)
```

```python
import functools
import math

import jax
import jax.numpy as jnp
from jax import lax
from jax.experimental import pallas as pl
from jax.experimental.pallas import tpu as pltpu

BF16 = jnp.bfloat16
F32 = jnp.float32

GRID_W = 64
RMS_EPS = 1e-6
GN_EPS = 1e-6
ROPE_THETA = 10000.0

RET_HEADS = 4
RET_QK_DIM = 256
RET_V_DIM = 512
RET_QK_W = RET_HEADS * RET_QK_DIM
RET_V_W = RET_HEADS * RET_V_DIM
RET_CHUNK = 256

ATTN_HEAD_DIM = 128
ATTN_Q_HEADS = 8
ATTN_KV_HEADS = 2
ATTN_GROUP = ATTN_Q_HEADS // ATTN_KV_HEADS
ATTN_Q_W = ATTN_Q_HEADS * ATTN_HEAD_DIM
ATTN_KV_W = ATTN_KV_HEADS * ATTN_HEAD_DIM
ATTN_IN_W = ATTN_Q_W + 2 * ATTN_KV_W

TOKEN_TILE = 512
ATTN_Q_TILE = 256
ATTN_Q_TILES_PER_STEP = 4
ATTN_KV_CHUNK = 256
ATTN_VT_ROWS = ATTN_HEAD_DIM + 16
ATTN_Q_SCALE = ATTN_HEAD_DIM ** -0.5 * math.log2(math.e)
FF_CHUNK = 1024
V7X_VMEM_LIMIT_BYTES = 56 * 1024 * 1024


def _dot(a, b):
    return jnp.dot(a, b, preferred_element_type=F32)


def _dot_nt(a, b):
    return lax.dot_general(a, b, (((1,), (1,)), ((), ())), preferred_element_type=F32)


def _rms_norm_rows(x, gain):
    ms = jnp.mean(x * x, axis=-1, keepdims=True)
    return x * lax.rsqrt(ms + RMS_EPS) * gain


def _resident(shape):
    nd = len(shape)
    return pl.BlockSpec(shape, lambda *_: (0,) * nd, pipeline_mode=pl.Buffered(1))


def _axial_rope_tables(seq, rot_dim):
    rows = seq // GRID_W
    row = jnp.broadcast_to(jnp.arange(rows, dtype=F32)[:, None], (rows, GRID_W)).reshape(seq)
    col = jnp.broadcast_to(jnp.arange(GRID_W, dtype=F32)[None, :], (rows, GRID_W)).reshape(seq)
    per_axis = rot_dim // 2
    n_freq = per_axis // 2
    inv_freq = ROPE_THETA ** (-jnp.arange(n_freq, dtype=F32) * 2.0 / per_axis)
    ang = jnp.concatenate([row[:, None] * inv_freq[None, :], col[:, None] * inv_freq[None, :]], axis=-1)
    return jnp.cos(ang), jnp.sin(ang)


def _ret_proj_kernel(h_ref, gain_ref, wq_ref, wkt_ref, wvg_ref, cos_ref, sin_ref, cost_ref, sint_ref,
                     q_ref, kt_ref, v_ref, gs_ref):
    u = _rms_norm_rows(h_ref[...], gain_ref[...]).astype(BF16)
    half = RET_QK_DIM // 2
    cos = cos_ref[...]
    sin = sin_ref[...]
    for hh in range(RET_HEADS):
        lo = hh * RET_QK_DIM
        qh = _dot(u, wq_ref[:, lo:lo + RET_QK_DIM])
        x1 = qh[:, :half]
        x2 = qh[:, half:]
        q_ref[:, lo:lo + half] = (x1 * cos - x2 * sin).astype(BF16)
        q_ref[:, lo + half:lo + RET_QK_DIM] = (x1 * sin + x2 * cos).astype(BF16)
    cos_t = cost_ref[...]
    sin_t = sint_ref[...]
    k_scale = RET_QK_DIM ** -0.5
    for hh in range(RET_HEADS):
        lo = hh * RET_QK_DIM
        kh = _dot_nt(wkt_ref[lo:lo + RET_QK_DIM, :], u)
        x1 = kh[:half]
        x2 = kh[half:]
        kt_ref[0, lo:lo + half, :] = ((x1 * cos_t - x2 * sin_t) * k_scale).astype(BF16)
        kt_ref[0, lo + half:lo + RET_QK_DIM, :] = ((x1 * sin_t + x2 * cos_t) * k_scale).astype(BF16)
    for c in range(RET_HEADS):
        lo = c * RET_V_DIM
        v_ref[:, lo:lo + RET_V_DIM] = _dot(u, wvg_ref[:, lo:lo + RET_V_DIM]).astype(BF16)
    for c in range(RET_HEADS):
        lo = c * RET_V_DIM
        g = _dot(u, wvg_ref[:, RET_V_W + lo:RET_V_W + lo + RET_V_DIM])
        gs_ref[:, lo:lo + RET_V_DIM] = (g * jax.nn.sigmoid(g)).astype(BF16)


def _ret_proj(h, gain, wq, wkt, wvg, cos, sin, batch, seq):
    tokens, d = h.shape
    tm = TOKEN_TILE
    tiles_per_seq = seq // tm
    half = RET_QK_DIM // 2
    cos_t = cos.T
    sin_t = sin.T
    row_tile = lambda width: pl.BlockSpec((tm, width), lambda i: (i, 0))
    return pl.pallas_call(
        _ret_proj_kernel,
        grid=(tokens // tm,),
        in_specs=[
            row_tile(d),
            _resident((1, d)),
            _resident(wq.shape),
            _resident(wkt.shape),
            _resident(wvg.shape),
            pl.BlockSpec((tm, half), lambda i: (i % tiles_per_seq, 0)),
            pl.BlockSpec((tm, half), lambda i: (i % tiles_per_seq, 0)),
            pl.BlockSpec((half, tm), lambda i: (0, i % tiles_per_seq)),
            pl.BlockSpec((half, tm), lambda i: (0, i % tiles_per_seq)),
        ],
        out_specs=[
            row_tile(RET_QK_W),
            pl.BlockSpec((1, RET_QK_W, tm), lambda i: (i // tiles_per_seq, 0, i % tiles_per_seq)),
            row_tile(RET_V_W),
            row_tile(RET_V_W),
        ],
        out_shape=[
            jax.ShapeDtypeStruct((tokens, RET_QK_W), BF16),
            jax.ShapeDtypeStruct((batch, RET_QK_W, seq), BF16),
            jax.ShapeDtypeStruct((tokens, RET_V_W), BF16),
            jax.ShapeDtypeStruct((tokens, RET_V_W), BF16),
        ],
        compiler_params=pltpu.CompilerParams(
            dimension_semantics=("arbitrary",), vmem_limit_bytes=V7X_VMEM_LIMIT_BYTES),
        name="ret_proj",
    )(h, gain, wq, wkt, wvg, cos, sin, cos_t, sin_t)


def _retention_kernel(dec_f_ref, dec_b_ref, q_ref, kt_ref, v_ref, gs_ref, y_ref, yacc_ref, state_ref):
    head = pl.program_id(1)
    c = RET_CHUNK
    n_chunks = q_ref.shape[1] // c
    dk = RET_QK_DIM
    lgf = -jnp.exp(jnp.full((c, c), dec_f_ref[head], F32))
    lgb = -jnp.exp(jnp.full((c, c), dec_b_ref[head], F32))
    row = lax.broadcasted_iota(jnp.int32, (c, c), 0).astype(F32)
    col = lax.broadcasted_iota(jnp.int32, (c, c), 1).astype(F32)
    diff = row - col
    decay = jnp.where(diff >= 0, jnp.exp(lgf * jnp.maximum(diff, 0.0)),
                      jnp.exp(lgb * jnp.maximum(-diff, 0.0)))
    xi_f = jnp.exp(lgf * (row + 1.0))
    xi_b = jnp.exp(lgb * (float(c) - row))
    zeta_f = jnp.exp(lgf * (float(c) - 1.0 - col))
    zeta_b = jnp.exp(lgb * col)
    state_shape = state_ref.shape
    gc_f = jnp.exp(-jnp.exp(jnp.full(state_shape, dec_f_ref[head], F32)) * float(c))
    gc_b = jnp.exp(-jnp.exp(jnp.full(state_shape, dec_b_ref[head], F32)) * float(c))

    def chunk(i):
        rows = slice(i * c, (i + 1) * c)
        return q_ref[0, rows, :], kt_ref[0, :, rows], v_ref[0, rows, :]

    state_ref[...] = jnp.zeros_like(state_ref)
    for i in range(n_chunks):
        qi, kti, vi = chunk(i)
        scores = _dot(qi, kti) * decay
        y = _dot(scores.astype(BF16), vi)
        if i > 0:
            qs = (qi.astype(F32) * xi_f).astype(BF16)
            y = y + _dot(qs, state_ref[...].astype(BF16))
        yacc_ref[i * c:(i + 1) * c, :] = y
        if i < n_chunks - 1:
            ks = (kti.astype(F32) * zeta_f).astype(BF16)
            state_ref[...] = state_ref[...] * gc_f + _dot(ks, vi)

    state_ref[...] = jnp.zeros_like(state_ref)
    for i in range(n_chunks - 1, -1, -1):
        qi, kti, vi = chunk(i)
        y = yacc_ref[i * c:(i + 1) * c, :]
        if i < n_chunks - 1:
            qs = (qi.astype(F32) * xi_b).astype(BF16)
            y = y + _dot(qs, state_ref[...].astype(BF16))
        mu = jnp.mean(y, axis=-1, keepdims=True)
        yc = y - mu
        var = jnp.mean(yc * yc, axis=-1, keepdims=True)
        yn = yc * lax.rsqrt(var + GN_EPS)
        y_ref[0, i * c:(i + 1) * c, :] = (gs_ref[0, i * c:(i + 1) * c, :].astype(F32) * yn).astype(BF16)
        if i > 0:
            ks = (kti.astype(F32) * zeta_b).astype(BF16)
            state_ref[...] = state_ref[...] * gc_b + _dot(ks, vi)


def _retention(q, kt, v, gs, dec_f, dec_b):
    batch, seq, _ = q.shape
    assert RET_CHUNK == RET_QK_DIM and seq % RET_CHUNK == 0
    grid_spec = pltpu.PrefetchScalarGridSpec(
        num_scalar_prefetch=2,
        grid=(batch, RET_HEADS),
        in_specs=[
            pl.BlockSpec((1, seq, RET_QK_DIM), lambda b, h, *_: (b, 0, h)),
            pl.BlockSpec((1, RET_QK_DIM, seq), lambda b, h, *_: (b, h, 0)),
            pl.BlockSpec((1, seq, RET_V_DIM), lambda b, h, *_: (b, 0, h)),
            pl.BlockSpec((1, seq, RET_V_DIM), lambda b, h, *_: (b, 0, h)),
        ],
        out_specs=pl.BlockSpec((1, seq, RET_V_DIM), lambda b, h, *_: (b, 0, h)),
        scratch_shapes=[pltpu.VMEM((seq, RET_V_DIM), F32), pltpu.VMEM((RET_QK_DIM, RET_V_DIM), F32)],
    )
    return pl.pallas_call(
        _retention_kernel,
        grid_spec=grid_spec,
        out_shape=jax.ShapeDtypeStruct((batch, seq, RET_V_W), BF16),
        compiler_params=pltpu.CompilerParams(
            dimension_semantics=("arbitrary", "arbitrary"), vmem_limit_bytes=V7X_VMEM_LIMIT_BYTES),
        name="retention",
    )(dec_f, dec_b, q, kt, v, gs)


def _mix_out_mlp(y_ref, h_ref, wo_ref, gmlp_ref, w1_ref, w2_ref):
    h1 = h_ref[...] + _dot(y_ref[...], wo_ref[...])
    u = _rms_norm_rows(h1, gmlp_ref[...]).astype(BF16)
    h2 = h1
    d_ff = w1_ref.shape[1]
    for c in range(d_ff // FF_CHUNK):
        lo = c * FF_CHUNK
        a = jnp.maximum(_dot(u, w1_ref[:, lo:lo + FF_CHUNK]), 0.0)
        h2 = h2 + _dot((a * a).astype(BF16), w2_ref[lo:lo + FF_CHUNK, :])
    return h2


def _tail_attn_proj_kernel(y_ref, h_ref, wo_ref, gmlp_ref, w1_ref, w2_ref,
                           gmix_ref, wq_ref, wk_ref, wvt_ref, qn_ref, kn_ref, cos2_ref, sin2_ref,
                           hout_ref, q_ref, k_ref, vt_ref):
    h2 = _mix_out_mlp(y_ref, h_ref, wo_ref, gmlp_ref, w1_ref, w2_ref)
    hout_ref[...] = h2
    u = _rms_norm_rows(h2, gmix_ref[...]).astype(BF16)
    cos2 = cos2_ref[...]
    sin2 = sin2_ref[...]
    d = ATTN_HEAD_DIM

    def norm_rope(x, gain):
        xn = _rms_norm_rows(x, gain)
        return xn * cos2 + pltpu.roll(xn, d // 2, 1) * sin2

    for hh in range(ATTN_Q_HEADS):
        lo = hh * d
        qh = _dot(u, wq_ref[:, lo:lo + d])
        q_ref[:, lo:lo + d] = (norm_rope(qh, qn_ref[...]) * ATTN_Q_SCALE).astype(BF16)
    for hh in range(ATTN_KV_HEADS):
        lo = hh * d
        kh = _dot(u, wk_ref[:, lo:lo + d])
        k_ref[:, lo:lo + d] = norm_rope(kh, kn_ref[...]).astype(BF16)
    vt = _dot_nt(wvt_ref[...], u)
    for hh in range(ATTN_KV_HEADS):
        vt_ref[0, hh, :d, :] = vt[hh * d:(hh + 1) * d].astype(BF16)
        vt_ref[0, hh, d:, :] = jnp.ones((ATTN_VT_ROWS - d, vt.shape[1]), BF16)


def _tail_final_kernel(y_ref, h_ref, wo_ref, gmlp_ref, w1_ref, w2_ref, gfin_ref, out_ref):
    h2 = _mix_out_mlp(y_ref, h_ref, wo_ref, gmlp_ref, w1_ref, w2_ref)
    out_ref[...] = _rms_norm_rows(h2, gfin_ref[...])


def _tail_common_specs(y, h, wo, w1, w2):
    tm = TOKEN_TILE
    d = h.shape[1]
    return [
        pl.BlockSpec((tm, y.shape[1]), lambda i: (i, 0)),
        pl.BlockSpec((tm, d), lambda i: (i, 0)),
        _resident(wo.shape),
        _resident((1, d)),
        _resident(w1.shape),
        _resident(w2.shape),
    ]


def _tail_attn_proj(y, h, wo, gmlp, w1, w2, gmix, wq, wk, wvt, qn, kn, cos2, sin2, batch, seq):
    tokens, d = h.shape
    tm = TOKEN_TILE
    tiles_per_seq = seq // tm
    pos_tile = pl.BlockSpec((tm, ATTN_HEAD_DIM), lambda i: (i % tiles_per_seq, 0))
    return pl.pallas_call(
        _tail_attn_proj_kernel,
        grid=(tokens // tm,),
        in_specs=_tail_common_specs(y, h, wo, w1, w2) + [
            _resident((1, d)),
            _resident(wq.shape),
            _resident(wk.shape),
            _resident(wvt.shape),
            _resident((1, ATTN_HEAD_DIM)),
            _resident((1, ATTN_HEAD_DIM)),
            pos_tile,
            pos_tile,
        ],
        out_specs=[
            pl.BlockSpec((tm, d), lambda i: (i, 0)),
            pl.BlockSpec((tm, ATTN_Q_W), lambda i: (i, 0)),
            pl.BlockSpec((tm, ATTN_KV_W), lambda i: (i, 0)),
            pl.BlockSpec((1, ATTN_KV_HEADS, ATTN_VT_ROWS, tm),
                         lambda i: (i // tiles_per_seq, 0, 0, i % tiles_per_seq)),
        ],
        out_shape=[
            jax.ShapeDtypeStruct((tokens, d), F32),
            jax.ShapeDtypeStruct((tokens, ATTN_Q_W), BF16),
            jax.ShapeDtypeStruct((tokens, ATTN_KV_W), BF16),
            jax.ShapeDtypeStruct((batch, ATTN_KV_HEADS, ATTN_VT_ROWS, seq), BF16),
        ],
        compiler_params=pltpu.CompilerParams(
            dimension_semantics=("arbitrary",), vmem_limit_bytes=V7X_VMEM_LIMIT_BYTES),
        name="tail_attn_proj",
    )(y, h, wo, gmlp, w1, w2, gmix, wq, wk, wvt, qn, kn, cos2, sin2)


def _tail_final(y, h, wo, gmlp, w1, w2, gfin):
    tokens, d = h.shape
    tm = TOKEN_TILE
    return pl.pallas_call(
        _tail_final_kernel,
        grid=(tokens // tm,),
        in_specs=_tail_common_specs(y, h, wo, w1, w2) + [_resident((1, d))],
        out_specs=pl.BlockSpec((tm, d), lambda i: (i, 0)),
        out_shape=jax.ShapeDtypeStruct((tokens, d), F32),
        compiler_params=pltpu.CompilerParams(
            dimension_semantics=("arbitrary",), vmem_limit_bytes=V7X_VMEM_LIMIT_BYTES),
        name="tail_final",
    )(y, h, wo, gmlp, w1, w2, gfin)


def _attention_kernel(q_ref, k_ref, vt_ref, o_ref, s_scr, p_scr):
    d = ATTN_HEAD_DIM
    seq = k_ref.shape[1]
    tq = ATTN_Q_TILE
    kc = ATTN_KV_CHUNK

    def q_tiles(t, carry):
        units = [(j, g) for j in range(ATTN_Q_TILES_PER_STEP) for g in range(ATTN_GROUP)]

        def rows(j):
            return pl.ds(pl.multiple_of((t * ATTN_Q_TILES_PER_STEP + j) * tq, tq), tq)

        def scores(u):
            j, g = units[u]
            q = q_ref[0, rows(j), g * d:(g + 1) * d]
            for c in range(seq // kc):
                s_scr[u % 2, c * kc:(c + 1) * kc, :] = _dot_nt(k_ref[0, c * kc:(c + 1) * kc, :], q)

        def softmax(u):
            m = None
            for c in range(seq // kc):
                mc = jnp.max(s_scr[u % 2, c * kc:(c + 1) * kc, :], axis=0, keepdims=True)
                m = mc if m is None else jnp.maximum(m, mc)
            for c in range(seq // kc):
                p = jnp.exp2(s_scr[u % 2, c * kc:(c + 1) * kc, :] - m)
                p_scr[u % 2, c * kc:(c + 1) * kc, :] = p.astype(BF16)

        def finish(u):
            j, g = units[u]
            acc = _dot(vt_ref[0, 0], p_scr[u % 2])
            o_ref[0, rows(j), g * d:(g + 1) * d] = (acc[:d] / acc[d:d + 1]).T.astype(BF16)

        scores(0)
        for u in range(len(units)):
            if u + 1 < len(units):
                scores(u + 1)
            if u > 0:
                finish(u - 1)
            softmax(u)
        finish(len(units) - 1)
        return carry

    lax.fori_loop(0, seq // (tq * ATTN_Q_TILES_PER_STEP), q_tiles, 0)


def _attention(q, k, vt):
    batch, seq, _ = q.shape
    group_w = ATTN_GROUP * ATTN_HEAD_DIM
    return pl.pallas_call(
        _attention_kernel,
        grid=(batch, ATTN_KV_HEADS),
        in_specs=[
            pl.BlockSpec((1, seq, group_w), lambda b, kk: (b, 0, kk)),
            pl.BlockSpec((1, seq, ATTN_HEAD_DIM), lambda b, kk: (b, 0, kk)),
            pl.BlockSpec((1, 1, ATTN_VT_ROWS, seq), lambda b, kk: (b, kk, 0, 0)),
        ],
        out_specs=pl.BlockSpec((1, seq, group_w), lambda b, kk: (b, 0, kk)),
        out_shape=jax.ShapeDtypeStruct((batch, seq, ATTN_Q_W), BF16),
        scratch_shapes=[pltpu.VMEM((2, seq, ATTN_Q_TILE), F32), pltpu.VMEM((2, seq, ATTN_Q_TILE), BF16)],
        compiler_params=pltpu.CompilerParams(
            dimension_semantics=("arbitrary", "arbitrary"),
            vmem_limit_bytes=V7X_VMEM_LIMIT_BYTES),
        name="attention",
    )(q, k, vt)


def kernel(x, norm_mix, norm_mlp, mlp_w1, mlp_w2, ret_w_in, ret_w_out, ret_decay_fwd, ret_decay_bwd,
           attn_w_in, attn_w_out, attn_q_norm, attn_k_norm, final_norm):
    batch, seq, d = x.shape
    assert norm_mix.shape[0] == 2 and ret_w_in.shape[0] == 1 and attn_w_in.shape[0] == 1
    assert seq % TOKEN_TILE == 0 and seq % ATTN_Q_TILE == 0 and seq % GRID_W == 0
    tokens = batch * seq
    h = x.reshape(tokens, d)

    cos_r, sin_r = _axial_rope_tables(seq, RET_QK_DIM)
    cos_a, sin_a = _axial_rope_tables(seq, ATTN_HEAD_DIM)
    cos2 = jnp.concatenate([cos_a, cos_a], axis=-1)
    sin2 = jnp.concatenate([-sin_a, sin_a], axis=-1)

    w_in = ret_w_in[0]
    wq = w_in[:, :RET_QK_W].astype(BF16)
    wkt = w_in[:, RET_QK_W:2 * RET_QK_W].T.astype(BF16)
    wvg = w_in[:, 2 * RET_QK_W:].astype(BF16)
    q, kt, v, gs = _ret_proj(h, norm_mix[0][None, :], wq, wkt, wvg, cos_r, sin_r, batch, seq)
    y = _retention(q.reshape(batch, seq, RET_QK_W), kt, v.reshape(batch, seq, RET_V_W),
                   gs.reshape(batch, seq, RET_V_W), ret_decay_fwd[0], ret_decay_bwd[0])
    wa = attn_w_in[0]
    h, qa, ka, vta = _tail_attn_proj(
        y.reshape(tokens, RET_V_W), h, ret_w_out[0].astype(BF16), norm_mlp[0][None, :],
        mlp_w1[0].astype(BF16), mlp_w2[0].astype(BF16), norm_mix[1][None, :],
        wa[:, :ATTN_Q_W].astype(BF16), wa[:, ATTN_Q_W:ATTN_Q_W + ATTN_KV_W].astype(BF16),
        wa[:, ATTN_Q_W + ATTN_KV_W:].T.astype(BF16),
        attn_q_norm[0][None, :], attn_k_norm[0][None, :], cos2, sin2, batch, seq)

    o = _attention(qa.reshape(batch, seq, ATTN_Q_W), ka.reshape(batch, seq, ATTN_KV_W), vta)
    out = _tail_final(o.reshape(tokens, d), h, attn_w_out[0].astype(BF16), norm_mlp[1][None, :],
                      mlp_w1[1].astype(BF16), mlp_w2[1].astype(BF16), final_norm[None, :])
    return out.reshape(batch, seq, d)
```

```python
import functools
import math

import jax
import jax.numpy as jnp
from jax import lax
from jax.experimental import pallas as pl
from jax.experimental.pallas import tpu as pltpu

BF16 = jnp.bfloat16
F32 = jnp.float32

GRID_W = 64
RMS_EPS = 1e-6
GN_EPS = 1e-6
ROPE_THETA = 10000.0

RET_HEADS = 4
RET_QK_DIM = 256
RET_V_DIM = 512
RET_QK_W = RET_HEADS * RET_QK_DIM
RET_V_W = RET_HEADS * RET_V_DIM
RET_CHUNK = 256

ATTN_HEAD_DIM = 128
ATTN_Q_HEADS = 8
ATTN_KV_HEADS = 2
ATTN_GROUP = ATTN_Q_HEADS // ATTN_KV_HEADS
ATTN_Q_W = ATTN_Q_HEADS * ATTN_HEAD_DIM
ATTN_KV_W = ATTN_KV_HEADS * ATTN_HEAD_DIM
ATTN_IN_W = ATTN_Q_W + 2 * ATTN_KV_W

TOKEN_TILE = 512
ATTN_Q_TILE = 256
ATTN_Q_TILES_PER_STEP = 4
ATTN_KV_CHUNK = 256
ATTN_VT_ROWS = ATTN_HEAD_DIM + 16
ATTN_Q_SCALE = ATTN_HEAD_DIM ** -0.5 * math.log2(math.e)
FF_CHUNK = 1024
V7X_VMEM_LIMIT_BYTES = 56 * 1024 * 1024


def _dot(a, b):
    return jnp.dot(a, b, preferred_element_type=F32)


def _dot_nt(a, b):
    return lax.dot_general(a, b, (((1,), (1,)), ((), ())), preferred_element_type=F32)


def _rms_scale(x):
    return lax.rsqrt(jnp.mean(x * x, axis=-1, keepdims=True) + RMS_EPS)


def _rms_norm_rows(x, gain):
    return x * _rms_scale(x) * gain


def _resident(shape):
    nd = len(shape)
    return pl.BlockSpec(shape, lambda *_: (0,) * nd, pipeline_mode=pl.Buffered(1))


def _axial_rope_tables(seq, rot_dim):
    rows = seq // GRID_W
    row = jnp.broadcast_to(jnp.arange(rows, dtype=F32)[:, None], (rows, GRID_W)).reshape(seq)
    col = jnp.broadcast_to(jnp.arange(GRID_W, dtype=F32)[None, :], (rows, GRID_W)).reshape(seq)
    per_axis = rot_dim // 2
    n_freq = per_axis // 2
    inv_freq = ROPE_THETA ** (-jnp.arange(n_freq, dtype=F32) * 2.0 / per_axis)
    ang = jnp.concatenate([row[:, None] * inv_freq[None, :], col[:, None] * inv_freq[None, :]], axis=-1)
    return jnp.cos(ang), jnp.sin(ang)


def _ret_proj_kernel(h_ref, gain_ref, wq_ref, wkt_ref, wvg_ref, cos_ref, sin_ref, cost_ref, sint_ref,
                     q_ref, kt_ref, v_ref, gs_ref):
    x = h_ref[...]
    u = (x * gain_ref[...]).astype(BF16)
    r = _rms_scale(x)
    half = RET_QK_DIM // 2
    r_t = jnp.broadcast_to(r, (x.shape[0], half)).T
    cos = cos_ref[...] * r
    sin = sin_ref[...] * r
    for hh in range(RET_HEADS):
        lo = hh * RET_QK_DIM
        qh = _dot(u, wq_ref[:, lo:lo + RET_QK_DIM])
        x1 = qh[:, :half]
        x2 = qh[:, half:]
        q_ref[:, lo:lo + half] = (x1 * cos - x2 * sin).astype(BF16)
        q_ref[:, lo + half:lo + RET_QK_DIM] = (x1 * sin + x2 * cos).astype(BF16)
    k_scale = RET_QK_DIM ** -0.5
    cos_t = cost_ref[...] * (r_t * k_scale)
    sin_t = sint_ref[...] * (r_t * k_scale)
    for hh in range(RET_HEADS):
        lo = hh * RET_QK_DIM
        kh = _dot_nt(wkt_ref[lo:lo + RET_QK_DIM, :], u)
        x1 = kh[:half]
        x2 = kh[half:]
        kt_ref[0, lo:lo + half, :] = (x1 * cos_t - x2 * sin_t).astype(BF16)
        kt_ref[0, lo + half:lo + RET_QK_DIM, :] = (x1 * sin_t + x2 * cos_t).astype(BF16)
    for c in range(RET_HEADS):
        lo = c * RET_V_DIM
        v_ref[:, lo:lo + RET_V_DIM] = (_dot(u, wvg_ref[:, lo:lo + RET_V_DIM]) * r).astype(BF16)
    for c in range(RET_HEADS):
        lo = c * RET_V_DIM
        g = _dot(u, wvg_ref[:, RET_V_W + lo:RET_V_W + lo + RET_V_DIM]) * r
        gs_ref[:, lo:lo + RET_V_DIM] = (g * jax.nn.sigmoid(g)).astype(BF16)


def _ret_proj(h, gain, wq, wkt, wvg, cos, sin, batch, seq):
    tokens, d = h.shape
    tm = TOKEN_TILE
    tiles_per_seq = seq // tm
    half = RET_QK_DIM // 2
    cos_t = cos.T
    sin_t = sin.T
    row_tile = lambda width: pl.BlockSpec((tm, width), lambda i: (i, 0))
    return pl.pallas_call(
        _ret_proj_kernel,
        grid=(tokens // tm,),
        in_specs=[
            row_tile(d),
            _resident((1, d)),
            _resident(wq.shape),
            _resident(wkt.shape),
            _resident(wvg.shape),
            pl.BlockSpec((tm, half), lambda i: (i % tiles_per_seq, 0)),
            pl.BlockSpec((tm, half), lambda i: (i % tiles_per_seq, 0)),
            pl.BlockSpec((half, tm), lambda i: (0, i % tiles_per_seq)),
            pl.BlockSpec((half, tm), lambda i: (0, i % tiles_per_seq)),
        ],
        out_specs=[
            row_tile(RET_QK_W),
            pl.BlockSpec((1, RET_QK_W, tm), lambda i: (i // tiles_per_seq, 0, i % tiles_per_seq)),
            row_tile(RET_V_W),
            row_tile(RET_V_W),
        ],
        out_shape=[
            jax.ShapeDtypeStruct((tokens, RET_QK_W), BF16),
            jax.ShapeDtypeStruct((batch, RET_QK_W, seq), BF16),
            jax.ShapeDtypeStruct((tokens, RET_V_W), BF16),
            jax.ShapeDtypeStruct((tokens, RET_V_W), BF16),
        ],
        compiler_params=pltpu.CompilerParams(
            dimension_semantics=("arbitrary",), vmem_limit_bytes=V7X_VMEM_LIMIT_BYTES),
        name="ret_proj",
    )(h, gain, wq, wkt, wvg, cos, sin, cos_t, sin_t)


def _retention_kernel(dec_f_ref, dec_b_ref, q_ref, kt_ref, v_ref, gs_ref, y_ref, yacc_ref, state_ref):
    head = pl.program_id(1)
    c = RET_CHUNK
    n_chunks = q_ref.shape[1] // c
    dk = RET_QK_DIM
    lgf = -jnp.exp(jnp.full((c, c), dec_f_ref[head], F32))
    lgb = -jnp.exp(jnp.full((c, c), dec_b_ref[head], F32))
    row = lax.broadcasted_iota(jnp.int32, (c, c), 0).astype(F32)
    col = lax.broadcasted_iota(jnp.int32, (c, c), 1).astype(F32)
    diff = row - col
    decay = jnp.where(diff >= 0, jnp.exp(lgf * jnp.maximum(diff, 0.0)),
                      jnp.exp(lgb * jnp.maximum(-diff, 0.0)))
    xi_f = jnp.exp(lgf * (row + 1.0))
    xi_b = jnp.exp(lgb * (float(c) - row))
    zeta_f = jnp.exp(lgf * (float(c) - 1.0 - col))
    zeta_b = jnp.exp(lgb * col)
    state_shape = state_ref.shape
    gc_f = jnp.exp(-jnp.exp(jnp.full(state_shape, dec_f_ref[head], F32)) * float(c))
    gc_b = jnp.exp(-jnp.exp(jnp.full(state_shape, dec_b_ref[head], F32)) * float(c))

    def chunk(i):
        rows = slice(i * c, (i + 1) * c)
        return q_ref[0, rows, :], kt_ref[0, :, rows], v_ref[0, rows, :]

    state_ref[...] = jnp.zeros_like(state_ref)
    for i in range(n_chunks):
        qi, kti, vi = chunk(i)
        scores = _dot(qi, kti) * decay
        y = _dot(scores.astype(BF16), vi)
        if i > 0:
            qs = (qi.astype(F32) * xi_f).astype(BF16)
            y = y + _dot(qs, state_ref[...].astype(BF16))
        yacc_ref[i * c:(i + 1) * c, :] = y
        if i < n_chunks - 1:
            ks = (kti.astype(F32) * zeta_f).astype(BF16)
            state_ref[...] = state_ref[...] * gc_f + _dot(ks, vi)

    state_ref[...] = jnp.zeros_like(state_ref)
    for i in range(n_chunks - 1, -1, -1):
        qi, kti, vi = chunk(i)
        y = yacc_ref[i * c:(i + 1) * c, :]
        if i < n_chunks - 1:
            qs = (qi.astype(F32) * xi_b).astype(BF16)
            y = y + _dot(qs, state_ref[...].astype(BF16))
        mu = jnp.mean(y, axis=-1, keepdims=True)
        yc = y - mu
        var = jnp.mean(yc * yc, axis=-1, keepdims=True)
        yn = yc * lax.rsqrt(var + GN_EPS)
        y_ref[0, i * c:(i + 1) * c, :] = (gs_ref[0, i * c:(i + 1) * c, :].astype(F32) * yn).astype(BF16)
        if i > 0:
            ks = (kti.astype(F32) * zeta_b).astype(BF16)
            state_ref[...] = state_ref[...] * gc_b + _dot(ks, vi)


def _retention(q, kt, v, gs, dec_f, dec_b):
    batch, seq, _ = q.shape
    assert RET_CHUNK == RET_QK_DIM and seq % RET_CHUNK == 0
    grid_spec = pltpu.PrefetchScalarGridSpec(
        num_scalar_prefetch=2,
        grid=(batch, RET_HEADS),
        in_specs=[
            pl.BlockSpec((1, seq, RET_QK_DIM), lambda b, h, *_: (b, 0, h)),
            pl.BlockSpec((1, RET_QK_DIM, seq), lambda b, h, *_: (b, h, 0)),
            pl.BlockSpec((1, seq, RET_V_DIM), lambda b, h, *_: (b, 0, h)),
            pl.BlockSpec((1, seq, RET_V_DIM), lambda b, h, *_: (b, 0, h)),
        ],
        out_specs=pl.BlockSpec((1, seq, RET_V_DIM), lambda b, h, *_: (b, 0, h)),
        scratch_shapes=[pltpu.VMEM((seq, RET_V_DIM), F32), pltpu.VMEM((RET_QK_DIM, RET_V_DIM), F32)],
    )
    return pl.pallas_call(
        _retention_kernel,
        grid_spec=grid_spec,
        out_shape=jax.ShapeDtypeStruct((batch, seq, RET_V_W), BF16),
        compiler_params=pltpu.CompilerParams(
            dimension_semantics=("arbitrary", "arbitrary"), vmem_limit_bytes=V7X_VMEM_LIMIT_BYTES),
        name="retention",
    )(dec_f, dec_b, q, kt, v, gs)


def _mix_out_mlp(y_ref, h_ref, wo_ref, gmlp_ref, w1_ref, w2_ref):
    h1 = h_ref[...] + _dot(y_ref[...], wo_ref[...])
    u = (h1 * gmlp_ref[...]).astype(BF16)
    r = _rms_scale(h1)
    mlp = None
    d_ff = w1_ref.shape[1]
    for c in range(d_ff // FF_CHUNK):
        lo = c * FF_CHUNK
        a = jnp.maximum(_dot(u, w1_ref[:, lo:lo + FF_CHUNK]), 0.0)
        part = _dot((a * a).astype(BF16), w2_ref[lo:lo + FF_CHUNK, :])
        mlp = part if mlp is None else mlp + part
    return h1 + (r * r) * mlp


def _tail_attn_proj_kernel(y_ref, h_ref, wo_ref, gmlp_ref, w1_ref, w2_ref,
                           gmix_ref, wqt_ref, wk_ref, wvt_ref, qn_ref, kn_ref,
                           cos2_ref, sin2_ref, cost_ref, sint_ref,
                           hout_ref, qt_ref, k_ref, vt_ref):
    h2 = _mix_out_mlp(y_ref, h_ref, wo_ref, gmlp_ref, w1_ref, w2_ref)
    hout_ref[...] = h2
    u = (h2 * gmix_ref[...]).astype(BF16)
    r = _rms_scale(h2)
    d = ATTN_HEAD_DIM
    half = d // 2
    r_t = jnp.broadcast_to(r, (h2.shape[0], d)).T[:1]

    cos_t = cost_ref[...]
    sin_t = sint_ref[...]
    gain_t = qn_ref[...]
    for pair in range(ATTN_Q_HEADS // 2):
        q2 = _dot_nt(wqt_ref[pair * 2 * d:(pair + 1) * 2 * d, :], u)
        for hh in range(2):
            x = q2[hh * d:(hh + 1) * d]
            ms = jnp.mean(x * x, axis=0, keepdims=True) * (r_t * r_t)
            xn = x * (lax.rsqrt(ms + RMS_EPS) * r_t * ATTN_Q_SCALE) * gain_t
            x1 = xn[:half]
            x2 = xn[half:]
            lo = (pair * 2 + hh) * d
            qt_ref[0, lo:lo + half, :] = (x1 * cos_t - x2 * sin_t).astype(BF16)
            qt_ref[0, lo + half:lo + d, :] = (x1 * sin_t + x2 * cos_t).astype(BF16)

    cos2 = cos2_ref[...]
    sin2 = sin2_ref[...]
    k2 = _dot(u, wk_ref[...]) * r
    for hh in range(ATTN_KV_HEADS):
        kn = _rms_norm_rows(k2[:, hh * d:(hh + 1) * d], kn_ref[...])
        k_ref[:, hh * d:(hh + 1) * d] = (kn * cos2 + pltpu.roll(kn, half, 1) * sin2).astype(BF16)
    vt = _dot_nt(wvt_ref[...], u)
    for hh in range(ATTN_KV_HEADS):
        vt_ref[0, hh, :d, :] = (vt[hh * d:(hh + 1) * d] * r_t).astype(BF16)
        vt_ref[0, hh, d:, :] = jnp.ones((ATTN_VT_ROWS - d, vt.shape[1]), BF16)


def _tail_final_kernel(y_ref, h_ref, wo_ref, gmlp_ref, w1_ref, w2_ref, gfin_ref, out_ref):
    h2 = _mix_out_mlp(y_ref, h_ref, wo_ref, gmlp_ref, w1_ref, w2_ref)
    out_ref[...] = _rms_norm_rows(h2, gfin_ref[...])


def _tail_common_specs(y, h, wo, w1, w2):
    tm = TOKEN_TILE
    d = h.shape[1]
    return [
        pl.BlockSpec((tm, y.shape[1]), lambda i: (i, 0)),
        pl.BlockSpec((tm, d), lambda i: (i, 0)),
        _resident(wo.shape),
        _resident((1, d)),
        _resident(w1.shape),
        _resident(w2.shape),
    ]


def _tail_attn_proj(y, h, wo, gmlp, w1, w2, gmix, wqt, wk, wvt, qn_t, kn, cos2, sin2, cos_t, sin_t,
                    batch, seq):
    tokens, d = h.shape
    tm = TOKEN_TILE
    tiles_per_seq = seq // tm
    half = ATTN_HEAD_DIM // 2
    pos_tile = pl.BlockSpec((tm, ATTN_HEAD_DIM), lambda i: (i % tiles_per_seq, 0))
    pos_tile_t = pl.BlockSpec((half, tm), lambda i: (0, i % tiles_per_seq))
    return pl.pallas_call(
        _tail_attn_proj_kernel,
        grid=(tokens // tm,),
        in_specs=_tail_common_specs(y, h, wo, w1, w2) + [
            _resident((1, d)),
            _resident(wqt.shape),
            _resident(wk.shape),
            _resident(wvt.shape),
            _resident(qn_t.shape),
            _resident((1, ATTN_HEAD_DIM)),
            pos_tile,
            pos_tile,
            pos_tile_t,
            pos_tile_t,
        ],
        out_specs=[
            pl.BlockSpec((tm, d), lambda i: (i, 0)),
            pl.BlockSpec((1, ATTN_Q_W, tm), lambda i: (i // tiles_per_seq, 0, i % tiles_per_seq)),
            pl.BlockSpec((tm, ATTN_KV_W), lambda i: (i, 0)),
            pl.BlockSpec((1, ATTN_KV_HEADS, ATTN_VT_ROWS, tm),
                         lambda i: (i // tiles_per_seq, 0, 0, i % tiles_per_seq)),
        ],
        out_shape=[
            jax.ShapeDtypeStruct((tokens, d), F32),
            jax.ShapeDtypeStruct((batch, ATTN_Q_W, seq), BF16),
            jax.ShapeDtypeStruct((tokens, ATTN_KV_W), BF16),
            jax.ShapeDtypeStruct((batch, ATTN_KV_HEADS, ATTN_VT_ROWS, seq), BF16),
        ],
        compiler_params=pltpu.CompilerParams(
            dimension_semantics=("arbitrary",), vmem_limit_bytes=V7X_VMEM_LIMIT_BYTES),
        name="tail_attn_proj",
    )(y, h, wo, gmlp, w1, w2, gmix, wqt, wk, wvt, qn_t, kn, cos2, sin2, cos_t, sin_t)


def _tail_final(y, h, wo, gmlp, w1, w2, gfin):
    tokens, d = h.shape
    tm = TOKEN_TILE
    return pl.pallas_call(
        _tail_final_kernel,
        grid=(tokens // tm,),
        in_specs=_tail_common_specs(y, h, wo, w1, w2) + [_resident((1, d))],
        out_specs=pl.BlockSpec((tm, d), lambda i: (i, 0)),
        out_shape=jax.ShapeDtypeStruct((tokens, d), F32),
        compiler_params=pltpu.CompilerParams(
            dimension_semantics=("arbitrary",), vmem_limit_bytes=V7X_VMEM_LIMIT_BYTES),
        name="tail_final",
    )(y, h, wo, gmlp, w1, w2, gfin)


def _attention_kernel(qt_ref, k_ref, vt_ref, o_ref, s_scr, p_scr):
    d = ATTN_HEAD_DIM
    seq = k_ref.shape[1]
    tq = ATTN_Q_TILE
    kc = ATTN_KV_CHUNK
    units = [(j, g) for j in range(ATTN_Q_TILES_PER_STEP) for g in range(ATTN_GROUP)]

    def scores(u):
        j, g = units[u]
        qt = qt_ref[0, g * d:(g + 1) * d, j * tq:(j + 1) * tq]
        for c in range(seq // kc):
            s_scr[u % 2, c * kc:(c + 1) * kc, :] = _dot(k_ref[0, c * kc:(c + 1) * kc, :], qt)

    def softmax(u):
        m = None
        for c in range(seq // kc):
            mc = jnp.max(s_scr[u % 2, c * kc:(c + 1) * kc, :], axis=0, keepdims=True)
            m = mc if m is None else jnp.maximum(m, mc)
        for c in range(seq // kc):
            p = jnp.exp2(s_scr[u % 2, c * kc:(c + 1) * kc, :] - m)
            p_scr[u % 2, c * kc:(c + 1) * kc, :] = p.astype(BF16)

    def finish(u):
        j, g = units[u]
        acc = _dot(vt_ref[0, 0], p_scr[u % 2])
        o_ref[0, j * tq:(j + 1) * tq, g * d:(g + 1) * d] = (acc[:d] / acc[d:d + 1]).T.astype(BF16)

    scores(0)
    for u in range(len(units)):
        if u + 1 < len(units):
            scores(u + 1)
        if u > 0:
            finish(u - 1)
        softmax(u)
    finish(len(units) - 1)


def _attention(qt, k, vt):
    batch, _, seq = qt.shape
    group_w = ATTN_GROUP * ATTN_HEAD_DIM
    tq_step = ATTN_Q_TILE * ATTN_Q_TILES_PER_STEP
    return pl.pallas_call(
        _attention_kernel,
        grid=(batch, ATTN_KV_HEADS, seq // tq_step),
        in_specs=[
            pl.BlockSpec((1, group_w, tq_step), lambda b, kk, t: (b, kk, t)),
            pl.BlockSpec((1, seq, ATTN_HEAD_DIM), lambda b, kk, t: (b, 0, kk)),
            pl.BlockSpec((1, 1, ATTN_VT_ROWS, seq), lambda b, kk, t: (b, kk, 0, 0)),
        ],
        out_specs=pl.BlockSpec((1, tq_step, group_w), lambda b, kk, t: (b, t, kk)),
        out_shape=jax.ShapeDtypeStruct((batch, seq, ATTN_Q_W), BF16),
        scratch_shapes=[pltpu.VMEM((2, seq, ATTN_Q_TILE), F32), pltpu.VMEM((2, seq, ATTN_Q_TILE), BF16)],
        compiler_params=pltpu.CompilerParams(
            dimension_semantics=("arbitrary", "arbitrary", "arbitrary"),
            vmem_limit_bytes=V7X_VMEM_LIMIT_BYTES),
        name="attention",
    )(qt, k, vt)


def kernel(x, norm_mix, norm_mlp, mlp_w1, mlp_w2, ret_w_in, ret_w_out, ret_decay_fwd, ret_decay_bwd,
           attn_w_in, attn_w_out, attn_q_norm, attn_k_norm, final_norm):
    batch, seq, d = x.shape
    assert norm_mix.shape[0] == 2 and ret_w_in.shape[0] == 1 and attn_w_in.shape[0] == 1
    assert seq % TOKEN_TILE == 0 and seq % ATTN_Q_TILE == 0 and seq % GRID_W == 0
    tokens = batch * seq
    h = x.reshape(tokens, d)

    cos_r, sin_r = _axial_rope_tables(seq, RET_QK_DIM)
    cos_a, sin_a = _axial_rope_tables(seq, ATTN_HEAD_DIM)
    cos2 = jnp.concatenate([cos_a, cos_a], axis=-1)
    sin2 = jnp.concatenate([-sin_a, sin_a], axis=-1)

    w_in = ret_w_in[0]
    wq = w_in[:, :RET_QK_W].astype(BF16)
    wkt = w_in[:, RET_QK_W:2 * RET_QK_W].T.astype(BF16)
    wvg = w_in[:, 2 * RET_QK_W:].astype(BF16)
    q, kt, v, gs = _ret_proj(h, norm_mix[0][None, :], wq, wkt, wvg, cos_r, sin_r, batch, seq)
    y = _retention(q.reshape(batch, seq, RET_QK_W), kt, v.reshape(batch, seq, RET_V_W),
                   gs.reshape(batch, seq, RET_V_W), ret_decay_fwd[0], ret_decay_bwd[0])
    wa = attn_w_in[0]
    h, qta, ka, vta = _tail_attn_proj(
        y.reshape(tokens, RET_V_W), h, ret_w_out[0].astype(BF16), norm_mlp[0][None, :],
        mlp_w1[0].astype(BF16), mlp_w2[0].astype(BF16), norm_mix[1][None, :],
        wa[:, :ATTN_Q_W].T.astype(BF16), wa[:, ATTN_Q_W:ATTN_Q_W + ATTN_KV_W].astype(BF16),
        wa[:, ATTN_Q_W + ATTN_KV_W:].T.astype(BF16),
        jnp.broadcast_to(attn_q_norm[0][:, None], (ATTN_HEAD_DIM, TOKEN_TILE)), attn_k_norm[0][None, :],
        cos2, sin2, cos_a.T, sin_a.T, batch, seq)

    o = _attention(qta, ka.reshape(batch, seq, ATTN_KV_W), vta)
    out = _tail_final(o.reshape(tokens, d), h, attn_w_out[0].astype(BF16), norm_mlp[1][None, :],
                      mlp_w1[1].astype(BF16), mlp_w2[1].astype(BF16), final_norm[None, :])
    return out.reshape(batch, seq, d)
```

```python
import functools
import math

import jax
import jax.numpy as jnp
from jax import lax
from jax.experimental import pallas as pl
from jax.experimental.pallas import tpu as pltpu

BF16 = jnp.bfloat16
F32 = jnp.float32

GRID_W = 64
RMS_EPS = 1e-6
GN_EPS = 1e-6
ROPE_THETA = 10000.0

RET_HEADS = 4
RET_QK_DIM = 256
RET_V_DIM = 512
RET_QK_W = RET_HEADS * RET_QK_DIM
RET_V_W = RET_HEADS * RET_V_DIM
RET_CHUNK = 256

ATTN_HEAD_DIM = 128
ATTN_Q_HEADS = 8
ATTN_KV_HEADS = 2
ATTN_GROUP = ATTN_Q_HEADS // ATTN_KV_HEADS
ATTN_Q_W = ATTN_Q_HEADS * ATTN_HEAD_DIM
ATTN_KV_W = ATTN_KV_HEADS * ATTN_HEAD_DIM
ATTN_IN_W = ATTN_Q_W + 2 * ATTN_KV_W

TOKEN_TILE = 512
ATTN_Q_TILE = 256
ATTN_Q_TILES_PER_STEP = 4
ATTN_KV_CHUNK = 256
ATTN_VT_ROWS = ATTN_HEAD_DIM + 16
ATTN_Q_SCALE = ATTN_HEAD_DIM ** -0.5 * math.log2(math.e)
FF_CHUNK = 1024
V7X_VMEM_LIMIT_BYTES = 56 * 1024 * 1024


def _dot(a, b):
    return jnp.dot(a, b, preferred_element_type=F32)


def _dot_nt(a, b):
    return lax.dot_general(a, b, (((1,), (1,)), ((), ())), preferred_element_type=F32)


def _rms_scale(x):
    return lax.rsqrt(jnp.mean(x * x, axis=-1, keepdims=True) + RMS_EPS)


def _rms_norm_rows(x, gain):
    return x * _rms_scale(x) * gain


def _resident(shape):
    nd = len(shape)
    return pl.BlockSpec(shape, lambda *_: (0,) * nd, pipeline_mode=pl.Buffered(1))


def _axial_rope_tables(seq, rot_dim):
    rows = seq // GRID_W
    row = jnp.broadcast_to(jnp.arange(rows, dtype=F32)[:, None], (rows, GRID_W)).reshape(seq)
    col = jnp.broadcast_to(jnp.arange(GRID_W, dtype=F32)[None, :], (rows, GRID_W)).reshape(seq)
    per_axis = rot_dim // 2
    n_freq = per_axis // 2
    inv_freq = ROPE_THETA ** (-jnp.arange(n_freq, dtype=F32) * 2.0 / per_axis)
    ang = jnp.concatenate([row[:, None] * inv_freq[None, :], col[:, None] * inv_freq[None, :]], axis=-1)
    return jnp.cos(ang), jnp.sin(ang)


def _ret_proj_kernel(h_ref, gain_ref, wq_ref, wkt_ref, wvg_ref, cos_ref, sin_ref, cost_ref, sint_ref,
                     q_ref, kt_ref, v_ref, gs_ref):
    x = h_ref[...]
    u = (x * gain_ref[...]).astype(BF16)
    r = _rms_scale(x)
    half = RET_QK_DIM // 2
    r_t = jnp.broadcast_to(r, (x.shape[0], half)).T
    cos = cos_ref[...] * r
    sin = sin_ref[...] * r
    for hh in range(RET_HEADS):
        lo = hh * RET_QK_DIM
        qh = _dot(u, wq_ref[:, lo:lo + RET_QK_DIM])
        x1 = qh[:, :half]
        x2 = qh[:, half:]
        q_ref[:, lo:lo + half] = (x1 * cos - x2 * sin).astype(BF16)
        q_ref[:, lo + half:lo + RET_QK_DIM] = (x1 * sin + x2 * cos).astype(BF16)
    k_scale = RET_QK_DIM ** -0.5
    cos_t = cost_ref[...] * (r_t * k_scale)
    sin_t = sint_ref[...] * (r_t * k_scale)
    for hh in range(RET_HEADS):
        lo = hh * RET_QK_DIM
        kh = _dot_nt(wkt_ref[lo:lo + RET_QK_DIM, :], u)
        x1 = kh[:half]
        x2 = kh[half:]
        kt_ref[0, lo:lo + half, :] = (x1 * cos_t - x2 * sin_t).astype(BF16)
        kt_ref[0, lo + half:lo + RET_QK_DIM, :] = (x1 * sin_t + x2 * cos_t).astype(BF16)
    for c in range(RET_HEADS):
        lo = c * RET_V_DIM
        v_ref[:, lo:lo + RET_V_DIM] = (_dot(u, wvg_ref[:, lo:lo + RET_V_DIM]) * r).astype(BF16)
    for c in range(RET_HEADS):
        lo = c * RET_V_DIM
        g = _dot(u, wvg_ref[:, RET_V_W + lo:RET_V_W + lo + RET_V_DIM]) * r
        gs_ref[:, lo:lo + RET_V_DIM] = (g * jax.nn.sigmoid(g)).astype(BF16)


def _ret_proj(h, gain, wq, wkt, wvg, cos, sin, batch, seq):
    tokens, d = h.shape
    tm = TOKEN_TILE
    tiles_per_seq = seq // tm
    half = RET_QK_DIM // 2
    cos_t = cos.T
    sin_t = sin.T
    row_tile = lambda width: pl.BlockSpec((tm, width), lambda i: (i, 0))
    return pl.pallas_call(
        _ret_proj_kernel,
        grid=(tokens // tm,),
        in_specs=[
            row_tile(d),
            _resident((1, d)),
            _resident(wq.shape),
            _resident(wkt.shape),
            _resident(wvg.shape),
            pl.BlockSpec((tm, half), lambda i: (i % tiles_per_seq, 0)),
            pl.BlockSpec((tm, half), lambda i: (i % tiles_per_seq, 0)),
            pl.BlockSpec((half, tm), lambda i: (0, i % tiles_per_seq)),
            pl.BlockSpec((half, tm), lambda i: (0, i % tiles_per_seq)),
        ],
        out_specs=[
            row_tile(RET_QK_W),
            pl.BlockSpec((1, RET_QK_W, tm), lambda i: (i // tiles_per_seq, 0, i % tiles_per_seq)),
            row_tile(RET_V_W),
            row_tile(RET_V_W),
        ],
        out_shape=[
            jax.ShapeDtypeStruct((tokens, RET_QK_W), BF16),
            jax.ShapeDtypeStruct((batch, RET_QK_W, seq), BF16),
            jax.ShapeDtypeStruct((tokens, RET_V_W), BF16),
            jax.ShapeDtypeStruct((tokens, RET_V_W), BF16),
        ],
        compiler_params=pltpu.CompilerParams(
            dimension_semantics=("arbitrary",), vmem_limit_bytes=V7X_VMEM_LIMIT_BYTES),
        name="ret_proj",
    )(h, gain, wq, wkt, wvg, cos, sin, cos_t, sin_t)


def _retention_kernel(dec_f_ref, dec_b_ref, q_ref, kt_ref, v_ref, gs_ref, y_ref, sf_scr, sb_scr):
    head = pl.program_id(1)
    c = RET_CHUNK
    n_chunks = q_ref.shape[1] // c
    lgf = -jnp.exp(jnp.full((c, c), dec_f_ref[head], F32))
    lgb = -jnp.exp(jnp.full((c, c), dec_b_ref[head], F32))
    row = lax.broadcasted_iota(jnp.int32, (c, c), 0).astype(F32)
    col = lax.broadcasted_iota(jnp.int32, (c, c), 1).astype(F32)
    diff = row - col
    decay = jnp.where(diff >= 0, jnp.exp(lgf * jnp.maximum(diff, 0.0)),
                      jnp.exp(lgb * jnp.maximum(-diff, 0.0)))
    xi_f = jnp.exp(lgf * (row + 1.0)).astype(BF16)
    xi_b = jnp.exp(lgb * (float(c) - row)).astype(BF16)
    zeta_f = jnp.exp(lgf * (float(c) - 1.0 - col)).astype(BF16)
    zeta_b = jnp.exp(lgb * col).astype(BF16)
    state_shape = sf_scr.shape[1:]
    gc_f = jnp.exp(-jnp.exp(jnp.full(state_shape, dec_f_ref[head], F32)) * float(c))
    gc_b = jnp.exp(-jnp.exp(jnp.full(state_shape, dec_b_ref[head], F32)) * float(c))

    def rows(i):
        return slice(i * c, (i + 1) * c)

    state = None
    for i in range(n_chunks - 1):
        upd = _dot(kt_ref[0, :, rows(i)] * zeta_f, v_ref[0, rows(i), :])
        state = upd if state is None else state * gc_f + upd
        sf_scr[i + 1] = state.astype(BF16)
    state = None
    for i in range(n_chunks - 1, 0, -1):
        upd = _dot(kt_ref[0, :, rows(i)] * zeta_b, v_ref[0, rows(i), :])
        state = upd if state is None else state * gc_b + upd
        sb_scr[i - 1] = state.astype(BF16)

    for i in range(n_chunks):
        qi = q_ref[0, rows(i), :]
        scores = _dot(qi, kt_ref[0, :, rows(i)]) * decay
        y = _dot(scores.astype(BF16), v_ref[0, rows(i), :])
        if i > 0:
            y = y + _dot(qi * xi_f, sf_scr[i])
        if i < n_chunks - 1:
            y = y + _dot(qi * xi_b, sb_scr[i])
        mu = jnp.mean(y, axis=-1, keepdims=True)
        yc = y - mu
        var = jnp.mean(yc * yc, axis=-1, keepdims=True)
        yn = (yc * lax.rsqrt(var + GN_EPS)).astype(BF16)
        y_ref[0, rows(i), :] = gs_ref[0, rows(i), :] * yn


def _retention(q, kt, v, gs, dec_f, dec_b):
    batch, seq, _ = q.shape
    assert RET_CHUNK == RET_QK_DIM and seq % RET_CHUNK == 0
    grid_spec = pltpu.PrefetchScalarGridSpec(
        num_scalar_prefetch=2,
        grid=(batch, RET_HEADS),
        in_specs=[
            pl.BlockSpec((1, seq, RET_QK_DIM), lambda b, h, *_: (b, 0, h)),
            pl.BlockSpec((1, RET_QK_DIM, seq), lambda b, h, *_: (b, h, 0)),
            pl.BlockSpec((1, seq, RET_V_DIM), lambda b, h, *_: (b, 0, h)),
            pl.BlockSpec((1, seq, RET_V_DIM), lambda b, h, *_: (b, 0, h)),
        ],
        out_specs=pl.BlockSpec((1, seq, RET_V_DIM), lambda b, h, *_: (b, 0, h)),
        scratch_shapes=[pltpu.VMEM((seq // RET_CHUNK, RET_QK_DIM, RET_V_DIM), BF16)] * 2,
    )
    return pl.pallas_call(
        _retention_kernel,
        grid_spec=grid_spec,
        out_shape=jax.ShapeDtypeStruct((batch, seq, RET_V_W), BF16),
        compiler_params=pltpu.CompilerParams(
            dimension_semantics=("arbitrary", "arbitrary"), vmem_limit_bytes=V7X_VMEM_LIMIT_BYTES),
        name="retention",
    )(dec_f, dec_b, q, kt, v, gs)


def _mix_out_mlp(y_ref, h_ref, wo_ref, gmlp_ref, w1_ref, w2_ref):
    h1 = h_ref[...] + _dot(y_ref[...], wo_ref[...])
    u = (h1 * gmlp_ref[...]).astype(BF16)
    r = _rms_scale(h1)
    mlp = None
    d_ff = w1_ref.shape[1]
    for c in range(d_ff // FF_CHUNK):
        lo = c * FF_CHUNK
        a = jnp.maximum(_dot(u, w1_ref[:, lo:lo + FF_CHUNK]), 0.0)
        part = _dot((a * a).astype(BF16), w2_ref[lo:lo + FF_CHUNK, :])
        mlp = part if mlp is None else mlp + part
    return h1 + (r * r) * mlp


def _tail_attn_proj_kernel(y_ref, h_ref, wo_ref, gmlp_ref, w1_ref, w2_ref,
                           gmix_ref, wqt_ref, wk_ref, wvt_ref, qn_ref, kn_ref,
                           cos2_ref, sin2_ref, cost_ref, sint_ref,
                           hout_ref, qt_ref, k_ref, vt_ref):
    h2 = _mix_out_mlp(y_ref, h_ref, wo_ref, gmlp_ref, w1_ref, w2_ref)
    hout_ref[...] = h2
    u = (h2 * gmix_ref[...]).astype(BF16)
    r = _rms_scale(h2)
    d = ATTN_HEAD_DIM
    half = d // 2
    r_t = jnp.broadcast_to(r, (h2.shape[0], d)).T[:1]

    cos_t = cost_ref[...]
    sin_t = sint_ref[...]
    gain_t = qn_ref[...]
    for pair in range(ATTN_Q_HEADS // 2):
        q2 = _dot_nt(wqt_ref[pair * 2 * d:(pair + 1) * 2 * d, :], u)
        for hh in range(2):
            x = q2[hh * d:(hh + 1) * d]
            ms = jnp.mean(x * x, axis=0, keepdims=True) * (r_t * r_t)
            xn = x * (lax.rsqrt(ms + RMS_EPS) * r_t * ATTN_Q_SCALE) * gain_t
            x1 = xn[:half]
            x2 = xn[half:]
            lo = (pair * 2 + hh) * d
            qt_ref[0, lo:lo + half, :] = (x1 * cos_t - x2 * sin_t).astype(BF16)
            qt_ref[0, lo + half:lo + d, :] = (x1 * sin_t + x2 * cos_t).astype(BF16)

    cos2 = cos2_ref[...]
    sin2 = sin2_ref[...]
    k2 = _dot(u, wk_ref[...]) * r
    for hh in range(ATTN_KV_HEADS):
        kn = _rms_norm_rows(k2[:, hh * d:(hh + 1) * d], kn_ref[...])
        k_ref[:, hh * d:(hh + 1) * d] = (kn * cos2 + pltpu.roll(kn, half, 1) * sin2).astype(BF16)
    vt = _dot_nt(wvt_ref[...], u)
    for hh in range(ATTN_KV_HEADS):
        vt_ref[0, hh, :d, :] = (vt[hh * d:(hh + 1) * d] * r_t).astype(BF16)
        vt_ref[0, hh, d:, :] = jnp.ones((ATTN_VT_ROWS - d, vt.shape[1]), BF16)


def _tail_final_kernel(y_ref, h_ref, wo_ref, gmlp_ref, w1_ref, w2_ref, gfin_ref, out_ref):
    h2 = _mix_out_mlp(y_ref, h_ref, wo_ref, gmlp_ref, w1_ref, w2_ref)
    out_ref[...] = _rms_norm_rows(h2, gfin_ref[...])


def _tail_common_specs(y, h, wo, w1, w2):
    tm = TOKEN_TILE
    d = h.shape[1]
    return [
        pl.BlockSpec((tm, y.shape[1]), lambda i: (i, 0)),
        pl.BlockSpec((tm, d), lambda i: (i, 0)),
        _resident(wo.shape),
        _resident((1, d)),
        _resident(w1.shape),
        _resident(w2.shape),
    ]


def _tail_attn_proj(y, h, wo, gmlp, w1, w2, gmix, wqt, wk, wvt, qn_t, kn, cos2, sin2, cos_t, sin_t,
                    batch, seq):
    tokens, d = h.shape
    tm = TOKEN_TILE
    tiles_per_seq = seq // tm
    half = ATTN_HEAD_DIM // 2
    pos_tile = pl.BlockSpec((tm, ATTN_HEAD_DIM), lambda i: (i % tiles_per_seq, 0))
    pos_tile_t = pl.BlockSpec((half, tm), lambda i: (0, i % tiles_per_seq))
    return pl.pallas_call(
        _tail_attn_proj_kernel,
        grid=(tokens // tm,),
        in_specs=_tail_common_specs(y, h, wo, w1, w2) + [
            _resident((1, d)),
            _resident(wqt.shape),
            _resident(wk.shape),
            _resident(wvt.shape),
            _resident(qn_t.shape),
            _resident((1, ATTN_HEAD_DIM)),
            pos_tile,
            pos_tile,
            pos_tile_t,
            pos_tile_t,
        ],
        out_specs=[
            pl.BlockSpec((tm, d), lambda i: (i, 0)),
            pl.BlockSpec((1, ATTN_Q_W, tm), lambda i: (i // tiles_per_seq, 0, i % tiles_per_seq)),
            pl.BlockSpec((tm, ATTN_KV_W), lambda i: (i, 0)),
            pl.BlockSpec((1, ATTN_KV_HEADS, ATTN_VT_ROWS, tm),
                         lambda i: (i // tiles_per_seq, 0, 0, i % tiles_per_seq)),
        ],
        out_shape=[
            jax.ShapeDtypeStruct((tokens, d), F32),
            jax.ShapeDtypeStruct((batch, ATTN_Q_W, seq), BF16),
            jax.ShapeDtypeStruct((tokens, ATTN_KV_W), BF16),
            jax.ShapeDtypeStruct((batch, ATTN_KV_HEADS, ATTN_VT_ROWS, seq), BF16),
        ],
        compiler_params=pltpu.CompilerParams(
            dimension_semantics=("arbitrary",), vmem_limit_bytes=V7X_VMEM_LIMIT_BYTES),
        name="tail_attn_proj",
    )(y, h, wo, gmlp, w1, w2, gmix, wqt, wk, wvt, qn_t, kn, cos2, sin2, cos_t, sin_t)


def _tail_final(y, h, wo, gmlp, w1, w2, gfin):
    tokens, d = h.shape
    tm = TOKEN_TILE
    return pl.pallas_call(
        _tail_final_kernel,
        grid=(tokens // tm,),
        in_specs=_tail_common_specs(y, h, wo, w1, w2) + [_resident((1, d))],
        out_specs=pl.BlockSpec((tm, d), lambda i: (i, 0)),
        out_shape=jax.ShapeDtypeStruct((tokens, d), F32),
        compiler_params=pltpu.CompilerParams(
            dimension_semantics=("arbitrary",), vmem_limit_bytes=V7X_VMEM_LIMIT_BYTES),
        name="tail_final",
    )(y, h, wo, gmlp, w1, w2, gfin)


def _attention_kernel(qt_ref, k_ref, vt_ref, o_ref, s_scr, p_scr):
    d = ATTN_HEAD_DIM
    seq = k_ref.shape[1]
    tq = ATTN_Q_TILE
    kc = ATTN_KV_CHUNK
    units = [(j, g) for j in range(ATTN_Q_TILES_PER_STEP) for g in range(ATTN_GROUP)]

    def scores(u):
        j, g = units[u]
        qt = qt_ref[0, g * d:(g + 1) * d, j * tq:(j + 1) * tq]
        for c in range(seq // kc):
            s_scr[u % 2, c * kc:(c + 1) * kc, :] = _dot(k_ref[0, c * kc:(c + 1) * kc, :], qt)

    def softmax(u):
        m = None
        for c in range(seq // kc):
            mc = jnp.max(s_scr[u % 2, c * kc:(c + 1) * kc, :], axis=0, keepdims=True)
            m = mc if m is None else jnp.maximum(m, mc)
        for c in range(seq // kc):
            p = jnp.exp2(s_scr[u % 2, c * kc:(c + 1) * kc, :] - m)
            p_scr[u % 2, c * kc:(c + 1) * kc, :] = p.astype(BF16)

    def finish(u):
        j, g = units[u]
        acc = _dot(vt_ref[0, 0], p_scr[u % 2])
        o_ref[0, j * tq:(j + 1) * tq, g * d:(g + 1) * d] = (acc[:d] / acc[d:d + 1]).T.astype(BF16)

    scores(0)
    for u in range(len(units)):
        if u + 1 < len(units):
            scores(u + 1)
        if u > 0:
            finish(u - 1)
        softmax(u)
    finish(len(units) - 1)


def _attention(qt, k, vt):
    batch, _, seq = qt.shape
    group_w = ATTN_GROUP * ATTN_HEAD_DIM
    tq_step = ATTN_Q_TILE * ATTN_Q_TILES_PER_STEP
    return pl.pallas_call(
        _attention_kernel,
        grid=(batch, ATTN_KV_HEADS, seq // tq_step),
        in_specs=[
            pl.BlockSpec((1, group_w, tq_step), lambda b, kk, t: (b, kk, t)),
            pl.BlockSpec((1, seq, ATTN_HEAD_DIM), lambda b, kk, t: (b, 0, kk)),
            pl.BlockSpec((1, 1, ATTN_VT_ROWS, seq), lambda b, kk, t: (b, kk, 0, 0)),
        ],
        out_specs=pl.BlockSpec((1, tq_step, group_w), lambda b, kk, t: (b, t, kk)),
        out_shape=jax.ShapeDtypeStruct((batch, seq, ATTN_Q_W), BF16),
        scratch_shapes=[pltpu.VMEM((2, seq, ATTN_Q_TILE), F32), pltpu.VMEM((2, seq, ATTN_Q_TILE), BF16)],
        compiler_params=pltpu.CompilerParams(
            dimension_semantics=("arbitrary", "arbitrary", "arbitrary"),
            vmem_limit_bytes=V7X_VMEM_LIMIT_BYTES),
        name="attention",
    )(qt, k, vt)


def kernel(x, norm_mix, norm_mlp, mlp_w1, mlp_w2, ret_w_in, ret_w_out, ret_decay_fwd, ret_decay_bwd,
           attn_w_in, attn_w_out, attn_q_norm, attn_k_norm, final_norm):
    batch, seq, d = x.shape
    assert norm_mix.shape[0] == 2 and ret_w_in.shape[0] == 1 and attn_w_in.shape[0] == 1
    assert seq % TOKEN_TILE == 0 and seq % ATTN_Q_TILE == 0 and seq % GRID_W == 0
    tokens = batch * seq
    h = x.reshape(tokens, d)

    cos_r, sin_r = _axial_rope_tables(seq, RET_QK_DIM)
    cos_a, sin_a = _axial_rope_tables(seq, ATTN_HEAD_DIM)
    cos2 = jnp.concatenate([cos_a, cos_a], axis=-1)
    sin2 = jnp.concatenate([-sin_a, sin_a], axis=-1)

    w_in = ret_w_in[0]
    wq = w_in[:, :RET_QK_W].astype(BF16)
    wkt = w_in[:, RET_QK_W:2 * RET_QK_W].T.astype(BF16)
    wvg = w_in[:, 2 * RET_QK_W:].astype(BF16)
    q, kt, v, gs = _ret_proj(h, norm_mix[0][None, :], wq, wkt, wvg, cos_r, sin_r, batch, seq)
    y = _retention(q.reshape(batch, seq, RET_QK_W), kt, v.reshape(batch, seq, RET_V_W),
                   gs.reshape(batch, seq, RET_V_W), ret_decay_fwd[0], ret_decay_bwd[0])
    wa = attn_w_in[0]
    h, qta, ka, vta = _tail_attn_proj(
        y.reshape(tokens, RET_V_W), h, ret_w_out[0].astype(BF16), norm_mlp[0][None, :],
        mlp_w1[0].astype(BF16), mlp_w2[0].astype(BF16), norm_mix[1][None, :],
        wa[:, :ATTN_Q_W].T.astype(BF16), wa[:, ATTN_Q_W:ATTN_Q_W + ATTN_KV_W].astype(BF16),
        wa[:, ATTN_Q_W + ATTN_KV_W:].T.astype(BF16),
        jnp.broadcast_to(attn_q_norm[0][:, None], (ATTN_HEAD_DIM, TOKEN_TILE)), attn_k_norm[0][None, :],
        cos2, sin2, cos_a.T, sin_a.T, batch, seq)

    o = _attention(qta, ka.reshape(batch, seq, ATTN_KV_W), vta)
    out = _tail_final(o.reshape(tokens, d), h, attn_w_out[0].astype(BF16), norm_mlp[1][None, :],
                      mlp_w1[1].astype(BF16), mlp_w2[1].astype(BF16), final_norm[None, :])
    return out.reshape(batch, seq, d)
```

```python
import functools
import math

import jax
import jax.numpy as jnp
from jax import lax
from jax.experimental import pallas as pl
from jax.experimental.pallas import tpu as pltpu

BF16 = jnp.bfloat16
F32 = jnp.float32

GRID_W = 64
RMS_EPS = 1e-6
GN_EPS = 1e-6
ROPE_THETA = 10000.0

RET_HEADS = 4
RET_QK_DIM = 256
RET_V_DIM = 512
RET_QK_W = RET_HEADS * RET_QK_DIM
RET_V_W = RET_HEADS * RET_V_DIM
RET_CHUNK = 256

ATTN_HEAD_DIM = 128
ATTN_Q_HEADS = 8
ATTN_KV_HEADS = 2
ATTN_GROUP = ATTN_Q_HEADS // ATTN_KV_HEADS
ATTN_Q_W = ATTN_Q_HEADS * ATTN_HEAD_DIM
ATTN_KV_W = ATTN_KV_HEADS * ATTN_HEAD_DIM
ATTN_IN_W = ATTN_Q_W + 2 * ATTN_KV_W

TOKEN_TILE = 512
ATTN_Q_TILE = 256
ATTN_Q_TILES_PER_STEP = 4
ATTN_KV_CHUNK = 256
ATTN_VT_ROWS = ATTN_HEAD_DIM + 16
ATTN_Q_SCALE = ATTN_HEAD_DIM ** -0.5 * math.log2(math.e)
ATTN_NOSHIFT_MAX_LOG2 = 64.0
FF_CHUNK = 1024
V7X_VMEM_LIMIT_BYTES = 56 * 1024 * 1024


def _dot(a, b):
    return jnp.dot(a, b, preferred_element_type=F32)


def _dot_nt(a, b):
    return lax.dot_general(a, b, (((1,), (1,)), ((), ())), preferred_element_type=F32)


def _rms_scale(x):
    return lax.rsqrt(jnp.mean(x * x, axis=-1, keepdims=True) + RMS_EPS)


def _rms_norm_rows(x, gain):
    return x * _rms_scale(x) * gain


def _resident(shape):
    nd = len(shape)
    return pl.BlockSpec(shape, lambda *_: (0,) * nd, pipeline_mode=pl.Buffered(1))


def _axial_rope_tables(seq, rot_dim):
    rows = seq // GRID_W
    row = jnp.broadcast_to(jnp.arange(rows, dtype=F32)[:, None], (rows, GRID_W)).reshape(seq)
    col = jnp.broadcast_to(jnp.arange(GRID_W, dtype=F32)[None, :], (rows, GRID_W)).reshape(seq)
    per_axis = rot_dim // 2
    n_freq = per_axis // 2
    inv_freq = ROPE_THETA ** (-jnp.arange(n_freq, dtype=F32) * 2.0 / per_axis)
    ang = jnp.concatenate([row[:, None] * inv_freq[None, :], col[:, None] * inv_freq[None, :]], axis=-1)
    return jnp.cos(ang), jnp.sin(ang)


def _ret_proj_kernel(h_ref, gain_ref, wq_ref, wkt_ref, wvg_ref, cos_ref, sin_ref, cost_ref, sint_ref,
                     q_ref, kt_ref, v_ref, gs_ref):
    x = h_ref[...]
    u = (x * gain_ref[...]).astype(BF16)
    r = _rms_scale(x)
    half = RET_QK_DIM // 2
    r_t = jnp.broadcast_to(r, (x.shape[0], half)).T
    cos = cos_ref[...] * r
    sin = sin_ref[...] * r
    for hh in range(RET_HEADS):
        lo = hh * RET_QK_DIM
        qh = _dot(u, wq_ref[:, lo:lo + RET_QK_DIM])
        x1 = qh[:, :half]
        x2 = qh[:, half:]
        q_ref[:, lo:lo + half] = (x1 * cos - x2 * sin).astype(BF16)
        q_ref[:, lo + half:lo + RET_QK_DIM] = (x1 * sin + x2 * cos).astype(BF16)
    k_scale = RET_QK_DIM ** -0.5
    cos_t = cost_ref[...] * (r_t * k_scale)
    sin_t = sint_ref[...] * (r_t * k_scale)
    for hh in range(RET_HEADS):
        lo = hh * RET_QK_DIM
        kh = _dot_nt(wkt_ref[lo:lo + RET_QK_DIM, :], u)
        x1 = kh[:half]
        x2 = kh[half:]
        kt_ref[0, lo:lo + half, :] = (x1 * cos_t - x2 * sin_t).astype(BF16)
        kt_ref[0, lo + half:lo + RET_QK_DIM, :] = (x1 * sin_t + x2 * cos_t).astype(BF16)
    for c in range(RET_HEADS):
        lo = c * RET_V_DIM
        v_ref[:, lo:lo + RET_V_DIM] = (_dot(u, wvg_ref[:, lo:lo + RET_V_DIM]) * r).astype(BF16)
    for c in range(RET_HEADS):
        lo = c * RET_V_DIM
        g = _dot(u, wvg_ref[:, RET_V_W + lo:RET_V_W + lo + RET_V_DIM]) * r
        gs_ref[:, lo:lo + RET_V_DIM] = (g * jax.nn.sigmoid(g)).astype(BF16)


def _ret_proj(h, gain, wq, wkt, wvg, cos, sin, batch, seq):
    tokens, d = h.shape
    tm = TOKEN_TILE
    tiles_per_seq = seq // tm
    half = RET_QK_DIM // 2
    cos_t = cos.T
    sin_t = sin.T
    row_tile = lambda width: pl.BlockSpec((tm, width), lambda i: (i, 0))
    return pl.pallas_call(
        _ret_proj_kernel,
        grid=(tokens // tm,),
        in_specs=[
            row_tile(d),
            _resident((1, d)),
            _resident(wq.shape),
            _resident(wkt.shape),
            _resident(wvg.shape),
            pl.BlockSpec((tm, half), lambda i: (i % tiles_per_seq, 0)),
            pl.BlockSpec((tm, half), lambda i: (i % tiles_per_seq, 0)),
            pl.BlockSpec((half, tm), lambda i: (0, i % tiles_per_seq)),
            pl.BlockSpec((half, tm), lambda i: (0, i % tiles_per_seq)),
        ],
        out_specs=[
            row_tile(RET_QK_W),
            pl.BlockSpec((1, RET_QK_W, tm), lambda i: (i // tiles_per_seq, 0, i % tiles_per_seq)),
            row_tile(RET_V_W),
            row_tile(RET_V_W),
        ],
        out_shape=[
            jax.ShapeDtypeStruct((tokens, RET_QK_W), BF16),
            jax.ShapeDtypeStruct((batch, RET_QK_W, seq), BF16),
            jax.ShapeDtypeStruct((tokens, RET_V_W), BF16),
            jax.ShapeDtypeStruct((tokens, RET_V_W), BF16),
        ],
        compiler_params=pltpu.CompilerParams(
            dimension_semantics=("arbitrary",), vmem_limit_bytes=V7X_VMEM_LIMIT_BYTES),
        name="ret_proj",
    )(h, gain, wq, wkt, wvg, cos, sin, cos_t, sin_t)


def _retention_kernel(dec_f_ref, dec_b_ref, q_ref, kt_ref, v_ref, gs_ref, y_ref, sf_scr, sb_scr):
    head = pl.program_id(1)
    c = RET_CHUNK
    n_chunks = q_ref.shape[1] // c
    lgf = -jnp.exp(jnp.full((c, c), dec_f_ref[head], F32))
    lgb = -jnp.exp(jnp.full((c, c), dec_b_ref[head], F32))
    row = lax.broadcasted_iota(jnp.int32, (c, c), 0).astype(F32)
    col = lax.broadcasted_iota(jnp.int32, (c, c), 1).astype(F32)
    diff = row - col
    decay = jnp.where(diff >= 0, jnp.exp(lgf * jnp.maximum(diff, 0.0)),
                      jnp.exp(lgb * jnp.maximum(-diff, 0.0)))
    xi_f = jnp.exp(lgf * (row + 1.0)).astype(BF16)
    xi_b = jnp.exp(lgb * (float(c) - row)).astype(BF16)
    zeta_f = jnp.exp(lgf * (float(c) - 1.0 - col)).astype(BF16)
    zeta_b = jnp.exp(lgb * col).astype(BF16)
    state_shape = sf_scr.shape[1:]
    gc_f = jnp.exp(-jnp.exp(jnp.full(state_shape, dec_f_ref[head], F32)) * float(c))
    gc_b = jnp.exp(-jnp.exp(jnp.full(state_shape, dec_b_ref[head], F32)) * float(c))

    def rows(i):
        return slice(i * c, (i + 1) * c)

    state = None
    for i in range(n_chunks - 1):
        upd = _dot(kt_ref[0, :, rows(i)] * zeta_f, v_ref[0, rows(i), :])
        state = upd if state is None else state * gc_f + upd
        sf_scr[i + 1] = state.astype(BF16)
    state = None
    for i in range(n_chunks - 1, 0, -1):
        upd = _dot(kt_ref[0, :, rows(i)] * zeta_b, v_ref[0, rows(i), :])
        state = upd if state is None else state * gc_b + upd
        sb_scr[i - 1] = state.astype(BF16)

    for i in range(n_chunks):
        qi = q_ref[0, rows(i), :]
        scores = _dot(qi, kt_ref[0, :, rows(i)]) * decay
        y = _dot(scores.astype(BF16), v_ref[0, rows(i), :])
        if i > 0:
            y = y + _dot(qi * xi_f, sf_scr[i])
        if i < n_chunks - 1:
            y = y + _dot(qi * xi_b, sb_scr[i])
        mu = jnp.mean(y, axis=-1, keepdims=True)
        yc = y - mu
        var = jnp.mean(yc * yc, axis=-1, keepdims=True)
        yn = (yc * lax.rsqrt(var + GN_EPS)).astype(BF16)
        y_ref[0, rows(i), :] = gs_ref[0, rows(i), :] * yn


def _retention(q, kt, v, gs, dec_f, dec_b):
    batch, seq, _ = q.shape
    assert RET_CHUNK == RET_QK_DIM and seq % RET_CHUNK == 0
    grid_spec = pltpu.PrefetchScalarGridSpec(
        num_scalar_prefetch=2,
        grid=(batch, RET_HEADS),
        in_specs=[
            pl.BlockSpec((1, seq, RET_QK_DIM), lambda b, h, *_: (b, 0, h)),
            pl.BlockSpec((1, RET_QK_DIM, seq), lambda b, h, *_: (b, h, 0)),
            pl.BlockSpec((1, seq, RET_V_DIM), lambda b, h, *_: (b, 0, h)),
            pl.BlockSpec((1, seq, RET_V_DIM), lambda b, h, *_: (b, 0, h)),
        ],
        out_specs=pl.BlockSpec((1, seq, RET_V_DIM), lambda b, h, *_: (b, 0, h)),
        scratch_shapes=[pltpu.VMEM((seq // RET_CHUNK, RET_QK_DIM, RET_V_DIM), BF16)] * 2,
    )
    return pl.pallas_call(
        _retention_kernel,
        grid_spec=grid_spec,
        out_shape=jax.ShapeDtypeStruct((batch, seq, RET_V_W), BF16),
        compiler_params=pltpu.CompilerParams(
            dimension_semantics=("arbitrary", "arbitrary"), vmem_limit_bytes=V7X_VMEM_LIMIT_BYTES),
        name="retention",
    )(dec_f, dec_b, q, kt, v, gs)


def _mix_out_mlp(y_ref, h_ref, wo_ref, gmlp_ref, w1_ref, w2_ref):
    h1 = h_ref[...] + _dot(y_ref[...], wo_ref[...])
    u = (h1 * gmlp_ref[...]).astype(BF16)
    r = _rms_scale(h1)
    mlp = None
    d_ff = w1_ref.shape[1]
    for c in range(d_ff // FF_CHUNK):
        lo = c * FF_CHUNK
        a = jnp.maximum(_dot(u, w1_ref[:, lo:lo + FF_CHUNK]), 0.0)
        part = _dot((a * a).astype(BF16), w2_ref[lo:lo + FF_CHUNK, :])
        mlp = part if mlp is None else mlp + part
    return h1 + (r * r) * mlp


def _tail_attn_proj_kernel(y_ref, h_ref, wo_ref, gmlp_ref, w1_ref, w2_ref,
                           gmix_ref, wqt_ref, wk_ref, wvt_ref, qn_ref, kn_ref,
                           cos2_ref, sin2_ref, cost_ref, sint_ref,
                           hout_ref, qt_ref, k_ref, vt_ref):
    h2 = _mix_out_mlp(y_ref, h_ref, wo_ref, gmlp_ref, w1_ref, w2_ref)
    hout_ref[...] = h2
    u = (h2 * gmix_ref[...]).astype(BF16)
    r = _rms_scale(h2)
    d = ATTN_HEAD_DIM
    half = d // 2
    r_t = jnp.broadcast_to(r, (h2.shape[0], d)).T[:1]

    cos_t = cost_ref[...]
    sin_t = sint_ref[...]
    gain_t = qn_ref[...]
    for pair in range(ATTN_Q_HEADS // 2):
        q2 = _dot_nt(wqt_ref[pair * 2 * d:(pair + 1) * 2 * d, :], u)
        for hh in range(2):
            x = q2[hh * d:(hh + 1) * d]
            ms = jnp.mean(x * x, axis=0, keepdims=True) * (r_t * r_t)
            xn = x * (lax.rsqrt(ms + RMS_EPS) * r_t * ATTN_Q_SCALE) * gain_t
            x1 = xn[:half]
            x2 = xn[half:]
            lo = (pair * 2 + hh) * d
            qt_ref[0, lo:lo + half, :] = (x1 * cos_t - x2 * sin_t).astype(BF16)
            qt_ref[0, lo + half:lo + d, :] = (x1 * sin_t + x2 * cos_t).astype(BF16)

    cos2 = cos2_ref[...]
    sin2 = sin2_ref[...]
    k2 = _dot(u, wk_ref[...]) * r
    for hh in range(ATTN_KV_HEADS):
        kn = _rms_norm_rows(k2[:, hh * d:(hh + 1) * d], kn_ref[...])
        k_ref[:, hh * d:(hh + 1) * d] = (kn * cos2 + pltpu.roll(kn, half, 1) * sin2).astype(BF16)
    vt = _dot_nt(wvt_ref[...], u)
    for hh in range(ATTN_KV_HEADS):
        vt_ref[0, hh, :d, :] = (vt[hh * d:(hh + 1) * d] * r_t).astype(BF16)
        vt_ref[0, hh, d:, :] = jnp.ones((ATTN_VT_ROWS - d, vt.shape[1]), BF16)


def _tail_final_kernel(y_ref, h_ref, wo_ref, gmlp_ref, w1_ref, w2_ref, gfin_ref, out_ref):
    h2 = _mix_out_mlp(y_ref, h_ref, wo_ref, gmlp_ref, w1_ref, w2_ref)
    out_ref[...] = _rms_norm_rows(h2, gfin_ref[...])


def _tail_common_specs(y, h, wo, w1, w2):
    tm = TOKEN_TILE
    d = h.shape[1]
    return [
        pl.BlockSpec((tm, y.shape[1]), lambda i: (i, 0)),
        pl.BlockSpec((tm, d), lambda i: (i, 0)),
        _resident(wo.shape),
        _resident((1, d)),
        _resident(w1.shape),
        _resident(w2.shape),
    ]


def _tail_attn_proj(y, h, wo, gmlp, w1, w2, gmix, wqt, wk, wvt, qn_t, kn, cos2, sin2, cos_t, sin_t,
                    batch, seq):
    tokens, d = h.shape
    tm = TOKEN_TILE
    tiles_per_seq = seq // tm
    half = ATTN_HEAD_DIM // 2
    pos_tile = pl.BlockSpec((tm, ATTN_HEAD_DIM), lambda i: (i % tiles_per_seq, 0))
    pos_tile_t = pl.BlockSpec((half, tm), lambda i: (0, i % tiles_per_seq))
    return pl.pallas_call(
        _tail_attn_proj_kernel,
        grid=(tokens // tm,),
        in_specs=_tail_common_specs(y, h, wo, w1, w2) + [
            _resident((1, d)),
            _resident(wqt.shape),
            _resident(wk.shape),
            _resident(wvt.shape),
            _resident(qn_t.shape),
            _resident((1, ATTN_HEAD_DIM)),
            pos_tile,
            pos_tile,
            pos_tile_t,
            pos_tile_t,
        ],
        out_specs=[
            pl.BlockSpec((tm, d), lambda i: (i, 0)),
            pl.BlockSpec((1, ATTN_Q_W, tm), lambda i: (i // tiles_per_seq, 0, i % tiles_per_seq)),
            pl.BlockSpec((tm, ATTN_KV_W), lambda i: (i, 0)),
            pl.BlockSpec((1, ATTN_KV_HEADS, ATTN_VT_ROWS, tm),
                         lambda i: (i // tiles_per_seq, 0, 0, i % tiles_per_seq)),
        ],
        out_shape=[
            jax.ShapeDtypeStruct((tokens, d), F32),
            jax.ShapeDtypeStruct((batch, ATTN_Q_W, seq), BF16),
            jax.ShapeDtypeStruct((tokens, ATTN_KV_W), BF16),
            jax.ShapeDtypeStruct((batch, ATTN_KV_HEADS, ATTN_VT_ROWS, seq), BF16),
        ],
        compiler_params=pltpu.CompilerParams(
            dimension_semantics=("arbitrary",), vmem_limit_bytes=V7X_VMEM_LIMIT_BYTES),
        name="tail_attn_proj",
    )(y, h, wo, gmlp, w1, w2, gmix, wqt, wk, wvt, qn_t, kn, cos2, sin2, cos_t, sin_t)


def _tail_final(y, h, wo, gmlp, w1, w2, gfin):
    tokens, d = h.shape
    tm = TOKEN_TILE
    return pl.pallas_call(
        _tail_final_kernel,
        grid=(tokens // tm,),
        in_specs=_tail_common_specs(y, h, wo, w1, w2) + [_resident((1, d))],
        out_specs=pl.BlockSpec((tm, d), lambda i: (i, 0)),
        out_shape=jax.ShapeDtypeStruct((tokens, d), F32),
        compiler_params=pltpu.CompilerParams(
            dimension_semantics=("arbitrary",), vmem_limit_bytes=V7X_VMEM_LIMIT_BYTES),
        name="tail_final",
    )(y, h, wo, gmlp, w1, w2, gfin)


def _attention_kernel(noshift_ref, qt_ref, k_ref, vt_ref, o_ref, s_scr, p_scr):
    d = ATTN_HEAD_DIM
    seq = k_ref.shape[1]
    tq = ATTN_Q_TILE
    kc = ATTN_KV_CHUNK
    units = [(j, g) for j in range(ATTN_Q_TILES_PER_STEP) for g in range(ATTN_GROUP)]

    def scores(u):
        j, g = units[u]
        qt = qt_ref[0, g * d:(g + 1) * d, j * tq:(j + 1) * tq]
        for c in range(seq // kc):
            s_scr[u % 2, c * kc:(c + 1) * kc, :] = _dot(k_ref[0, c * kc:(c + 1) * kc, :], qt)

    def softmax(u):
        m = None
        for c in range(seq // kc):
            mc = jnp.max(s_scr[u % 2, c * kc:(c + 1) * kc, :], axis=0, keepdims=True)
            m = mc if m is None else jnp.maximum(m, mc)
        for c in range(seq // kc):
            p = jnp.exp2(s_scr[u % 2, c * kc:(c + 1) * kc, :] - m)
            p_scr[u % 2, c * kc:(c + 1) * kc, :] = p.astype(BF16)

    def finish(u):
        j, g = units[u]
        acc = _dot(vt_ref[0, 0], p_scr[u % 2])
        o_ref[0, j * tq:(j + 1) * tq, g * d:(g + 1) * d] = (acc[:d] / acc[d:d + 1]).T.astype(BF16)

    def scores_exp(u):
        j, g = units[u]
        qt = qt_ref[0, g * d:(g + 1) * d, j * tq:(j + 1) * tq]
        for c in range(seq // kc):
            s = _dot(k_ref[0, c * kc:(c + 1) * kc, :], qt)
            p_scr[u % 2, c * kc:(c + 1) * kc, :] = jnp.exp2(s).astype(BF16)

    @pl.when(noshift_ref[0] == 1)
    def _():
        scores_exp(0)
        for u in range(len(units)):
            if u + 1 < len(units):
                scores_exp(u + 1)
            finish(u)

    @pl.when(noshift_ref[0] != 1)
    def _():
        scores(0)
        for u in range(len(units)):
            if u + 1 < len(units):
                scores(u + 1)
            if u > 0:
                finish(u - 1)
            softmax(u)
        finish(len(units) - 1)


def _attention(noshift, qt, k, vt):
    batch, _, seq = qt.shape
    group_w = ATTN_GROUP * ATTN_HEAD_DIM
    tq_step = ATTN_Q_TILE * ATTN_Q_TILES_PER_STEP
    grid_spec = pltpu.PrefetchScalarGridSpec(
        num_scalar_prefetch=1,
        grid=(batch, ATTN_KV_HEADS, seq // tq_step),
        in_specs=[
            pl.BlockSpec((1, group_w, tq_step), lambda b, kk, t, *_: (b, kk, t)),
            pl.BlockSpec((1, seq, ATTN_HEAD_DIM), lambda b, kk, t, *_: (b, 0, kk)),
            pl.BlockSpec((1, 1, ATTN_VT_ROWS, seq), lambda b, kk, t, *_: (b, kk, 0, 0)),
        ],
        out_specs=pl.BlockSpec((1, tq_step, group_w), lambda b, kk, t, *_: (b, t, kk)),
        scratch_shapes=[pltpu.VMEM((2, seq, ATTN_Q_TILE), F32), pltpu.VMEM((2, seq, ATTN_Q_TILE), BF16)],
    )
    return pl.pallas_call(
        _attention_kernel,
        grid_spec=grid_spec,
        out_shape=jax.ShapeDtypeStruct((batch, seq, ATTN_Q_W), BF16),
        compiler_params=pltpu.CompilerParams(
            dimension_semantics=("arbitrary", "arbitrary", "arbitrary"),
            vmem_limit_bytes=V7X_VMEM_LIMIT_BYTES),
        name="attention",
    )(noshift, qt, k, vt)


def kernel(x, norm_mix, norm_mlp, mlp_w1, mlp_w2, ret_w_in, ret_w_out, ret_decay_fwd, ret_decay_bwd,
           attn_w_in, attn_w_out, attn_q_norm, attn_k_norm, final_norm):
    batch, seq, d = x.shape
    assert norm_mix.shape[0] == 2 and ret_w_in.shape[0] == 1 and attn_w_in.shape[0] == 1
    assert seq % TOKEN_TILE == 0 and seq % ATTN_Q_TILE == 0 and seq % GRID_W == 0
    tokens = batch * seq
    h = x.reshape(tokens, d)

    cos_r, sin_r = _axial_rope_tables(seq, RET_QK_DIM)
    cos_a, sin_a = _axial_rope_tables(seq, ATTN_HEAD_DIM)
    cos2 = jnp.concatenate([cos_a, cos_a], axis=-1)
    sin2 = jnp.concatenate([-sin_a, sin_a], axis=-1)

    w_in = ret_w_in[0]
    wq = w_in[:, :RET_QK_W].astype(BF16)
    wkt = w_in[:, RET_QK_W:2 * RET_QK_W].T.astype(BF16)
    wvg = w_in[:, 2 * RET_QK_W:].astype(BF16)
    q, kt, v, gs = _ret_proj(h, norm_mix[0][None, :], wq, wkt, wvg, cos_r, sin_r, batch, seq)
    y = _retention(q.reshape(batch, seq, RET_QK_W), kt, v.reshape(batch, seq, RET_V_W),
                   gs.reshape(batch, seq, RET_V_W), ret_decay_fwd[0], ret_decay_bwd[0])
    wa = attn_w_in[0]
    h, qta, ka, vta = _tail_attn_proj(
        y.reshape(tokens, RET_V_W), h, ret_w_out[0].astype(BF16), norm_mlp[0][None, :],
        mlp_w1[0].astype(BF16), mlp_w2[0].astype(BF16), norm_mix[1][None, :],
        wa[:, :ATTN_Q_W].T.astype(BF16), wa[:, ATTN_Q_W:ATTN_Q_W + ATTN_KV_W].astype(BF16),
        wa[:, ATTN_Q_W + ATTN_KV_W:].T.astype(BF16),
        jnp.broadcast_to(attn_q_norm[0][:, None], (ATTN_HEAD_DIM, TOKEN_TILE)), attn_k_norm[0][None, :],
        cos2, sin2, cos_a.T, sin_a.T, batch, seq)

    score_bound = (ATTN_HEAD_DIM * ATTN_Q_SCALE * 1.01) * (
        jnp.max(jnp.abs(attn_q_norm[0])) * jnp.max(jnp.abs(attn_k_norm[0])))
    noshift = (score_bound <= ATTN_NOSHIFT_MAX_LOG2).astype(jnp.int32).reshape(1)
    o = _attention(noshift, qta, ka.reshape(batch, seq, ATTN_KV_W), vta)
    out = _tail_final(o.reshape(tokens, d), h, attn_w_out[0].astype(BF16), norm_mlp[1][None, :],
                      mlp_w1[1].astype(BF16), mlp_w2[1].astype(BF16), final_norm[None, :])
    return out.reshape(batch, seq, d)
```

```python
import functools
import math

import jax
import jax.numpy as jnp
from jax import lax
from jax.experimental import pallas as pl
from jax.experimental.pallas import tpu as pltpu

BF16 = jnp.bfloat16
F32 = jnp.float32

GRID_W = 64
RMS_EPS = 1e-6
GN_EPS = 1e-6
ROPE_THETA = 10000.0

RET_HEADS = 4
RET_QK_DIM = 256
RET_V_DIM = 512
RET_QK_W = RET_HEADS * RET_QK_DIM
RET_V_W = RET_HEADS * RET_V_DIM
RET_CHUNK = 256

ATTN_HEAD_DIM = 128
ATTN_Q_HEADS = 8
ATTN_KV_HEADS = 2
ATTN_GROUP = ATTN_Q_HEADS // ATTN_KV_HEADS
ATTN_Q_W = ATTN_Q_HEADS * ATTN_HEAD_DIM
ATTN_KV_W = ATTN_KV_HEADS * ATTN_HEAD_DIM
ATTN_IN_W = ATTN_Q_W + 2 * ATTN_KV_W

TOKEN_TILE = 512
ATTN_Q_TILE = 256
ATTN_Q_TILES_PER_STEP = 4
ATTN_KV_CHUNK = 256
ATTN_VT_ROWS = ATTN_HEAD_DIM + 16
ATTN_Q_SCALE = ATTN_HEAD_DIM ** -0.5 * math.log2(math.e)
ATTN_NOSHIFT_MAX_LOG2 = 64.0
FF_CHUNK = 1024
V7X_VMEM_LIMIT_BYTES = 56 * 1024 * 1024


def _dot(a, b):
    return jnp.dot(a, b, preferred_element_type=F32)


def _dot_nt(a, b):
    return lax.dot_general(a, b, (((1,), (1,)), ((), ())), preferred_element_type=F32)


def _rms_scale(x):
    return lax.rsqrt(jnp.mean(x * x, axis=-1, keepdims=True) + RMS_EPS)


def _rms_norm_rows(x, gain):
    return x * _rms_scale(x) * gain


def _resident(shape):
    nd = len(shape)
    return pl.BlockSpec(shape, lambda *_: (0,) * nd, pipeline_mode=pl.Buffered(1))


def _axial_rope_tables(seq, rot_dim):
    rows = seq // GRID_W
    row = jnp.broadcast_to(jnp.arange(rows, dtype=F32)[:, None], (rows, GRID_W)).reshape(seq)
    col = jnp.broadcast_to(jnp.arange(GRID_W, dtype=F32)[None, :], (rows, GRID_W)).reshape(seq)
    per_axis = rot_dim // 2
    n_freq = per_axis // 2
    inv_freq = ROPE_THETA ** (-jnp.arange(n_freq, dtype=F32) * 2.0 / per_axis)
    ang = jnp.concatenate([row[:, None] * inv_freq[None, :], col[:, None] * inv_freq[None, :]], axis=-1)
    return jnp.cos(ang), jnp.sin(ang)


def _ret_proj_kernel(h_ref, gain_ref, wq_ref, wkt_ref, wvg_ref, cos_ref, sin_ref, cost_ref, sint_ref,
                     q_ref, kt_ref, v_ref, gs_ref):
    x = h_ref[...]
    u = (x * gain_ref[...]).astype(BF16)
    r = _rms_scale(x)
    half = RET_QK_DIM // 2
    r_t = jnp.broadcast_to(r, (x.shape[0], half)).T
    for c in range(RET_HEADS):
        lo = c * RET_V_DIM
        g = _dot(u, wvg_ref[:, RET_V_W + lo:RET_V_W + lo + RET_V_DIM]) * r
        gs_ref[:, lo:lo + RET_V_DIM] = (g * jax.nn.sigmoid(g)).astype(BF16)
    k_scale = RET_QK_DIM ** -0.5
    cos_t = cost_ref[...] * (r_t * k_scale)
    sin_t = sint_ref[...] * (r_t * k_scale)
    for hh in range(RET_HEADS):
        lo = hh * RET_QK_DIM
        kh = _dot_nt(wkt_ref[lo:lo + RET_QK_DIM, :], u)
        x1 = kh[:half]
        x2 = kh[half:]
        kt_ref[0, lo:lo + half, :] = (x1 * cos_t - x2 * sin_t).astype(BF16)
        kt_ref[0, lo + half:lo + RET_QK_DIM, :] = (x1 * sin_t + x2 * cos_t).astype(BF16)
    cos = cos_ref[...] * r
    sin = sin_ref[...] * r
    for hh in range(RET_HEADS):
        lo = hh * RET_QK_DIM
        qh = _dot(u, wq_ref[:, lo:lo + RET_QK_DIM])
        x1 = qh[:, :half]
        x2 = qh[:, half:]
        q_ref[:, lo:lo + half] = (x1 * cos - x2 * sin).astype(BF16)
        q_ref[:, lo + half:lo + RET_QK_DIM] = (x1 * sin + x2 * cos).astype(BF16)
    for c in range(RET_HEADS):
        lo = c * RET_V_DIM
        v_ref[:, lo:lo + RET_V_DIM] = (_dot(u, wvg_ref[:, lo:lo + RET_V_DIM]) * r).astype(BF16)


def _ret_proj(h, gain, wq, wkt, wvg, cos, sin, batch, seq):
    tokens, d = h.shape
    tm = TOKEN_TILE
    tiles_per_seq = seq // tm
    half = RET_QK_DIM // 2
    cos_t = cos.T
    sin_t = sin.T
    row_tile = lambda width: pl.BlockSpec((tm, width), lambda i: (i, 0))
    return pl.pallas_call(
        _ret_proj_kernel,
        grid=(tokens // tm,),
        in_specs=[
            row_tile(d),
            _resident((1, d)),
            _resident(wq.shape),
            _resident(wkt.shape),
            _resident(wvg.shape),
            pl.BlockSpec((tm, half), lambda i: (i % tiles_per_seq, 0)),
            pl.BlockSpec((tm, half), lambda i: (i % tiles_per_seq, 0)),
            pl.BlockSpec((half, tm), lambda i: (0, i % tiles_per_seq)),
            pl.BlockSpec((half, tm), lambda i: (0, i % tiles_per_seq)),
        ],
        out_specs=[
            row_tile(RET_QK_W),
            pl.BlockSpec((1, RET_QK_W, tm), lambda i: (i // tiles_per_seq, 0, i % tiles_per_seq)),
            row_tile(RET_V_W),
            row_tile(RET_V_W),
        ],
        out_shape=[
            jax.ShapeDtypeStruct((tokens, RET_QK_W), BF16),
            jax.ShapeDtypeStruct((batch, RET_QK_W, seq), BF16),
            jax.ShapeDtypeStruct((tokens, RET_V_W), BF16),
            jax.ShapeDtypeStruct((tokens, RET_V_W), BF16),
        ],
        compiler_params=pltpu.CompilerParams(
            dimension_semantics=("arbitrary",), vmem_limit_bytes=V7X_VMEM_LIMIT_BYTES),
        name="ret_proj",
    )(h, gain, wq, wkt, wvg, cos, sin, cos_t, sin_t)


def _retention_kernel(dec_f_ref, dec_b_ref, q_ref, kt_ref, v_ref, gs_ref, y_ref, sf_scr, sb_scr):
    head = pl.program_id(1)
    c = RET_CHUNK
    n_chunks = q_ref.shape[1] // c
    lgf = -jnp.exp(jnp.full((c, c), dec_f_ref[head], F32))
    lgb = -jnp.exp(jnp.full((c, c), dec_b_ref[head], F32))
    row = lax.broadcasted_iota(jnp.int32, (c, c), 0).astype(F32)
    col = lax.broadcasted_iota(jnp.int32, (c, c), 1).astype(F32)
    diff = row - col
    decay = jnp.where(diff >= 0, jnp.exp(lgf * jnp.maximum(diff, 0.0)),
                      jnp.exp(lgb * jnp.maximum(-diff, 0.0)))
    xi_f = jnp.exp(lgf * (row + 1.0)).astype(BF16)
    xi_b = jnp.exp(lgb * (float(c) - row)).astype(BF16)
    zeta_f = jnp.exp(lgf * (float(c) - 1.0 - col)).astype(BF16)
    zeta_b = jnp.exp(lgb * col).astype(BF16)
    state_shape = sf_scr.shape[1:]
    gc_f = jnp.exp(-jnp.exp(jnp.full(state_shape, dec_f_ref[head], F32)) * float(c))
    gc_b = jnp.exp(-jnp.exp(jnp.full(state_shape, dec_b_ref[head], F32)) * float(c))

    def rows(i):
        return slice(i * c, (i + 1) * c)

    state = None
    for i in range(n_chunks - 1):
        upd = _dot(kt_ref[0, :, rows(i)] * zeta_f, v_ref[0, rows(i), :])
        state = upd if state is None else state * gc_f + upd
        sf_scr[i + 1] = state.astype(BF16)
    state = None
    for i in range(n_chunks - 1, 0, -1):
        upd = _dot(kt_ref[0, :, rows(i)] * zeta_b, v_ref[0, rows(i), :])
        state = upd if state is None else state * gc_b + upd
        sb_scr[i - 1] = state.astype(BF16)

    for i in range(n_chunks):
        qi = q_ref[0, rows(i), :]
        scores = _dot(qi, kt_ref[0, :, rows(i)]) * decay
        y = _dot(scores.astype(BF16), v_ref[0, rows(i), :])
        if i > 0:
            y = y + _dot(qi * xi_f, sf_scr[i])
        if i < n_chunks - 1:
            y = y + _dot(qi * xi_b, sb_scr[i])
        mu = jnp.mean(y, axis=-1, keepdims=True)
        yc = y - mu
        var = jnp.mean(yc * yc, axis=-1, keepdims=True)
        yn = (yc * lax.rsqrt(var + GN_EPS)).astype(BF16)
        y_ref[0, rows(i), :] = gs_ref[0, rows(i), :] * yn


def _retention(q, kt, v, gs, dec_f, dec_b):
    batch, seq, _ = q.shape
    assert RET_CHUNK == RET_QK_DIM and seq % RET_CHUNK == 0
    grid_spec = pltpu.PrefetchScalarGridSpec(
        num_scalar_prefetch=2,
        grid=(batch, RET_HEADS),
        in_specs=[
            pl.BlockSpec((1, seq, RET_QK_DIM), lambda b, h, *_: (b, 0, h)),
            pl.BlockSpec((1, RET_QK_DIM, seq), lambda b, h, *_: (b, h, 0)),
            pl.BlockSpec((1, seq, RET_V_DIM), lambda b, h, *_: (b, 0, h)),
            pl.BlockSpec((1, seq, RET_V_DIM), lambda b, h, *_: (b, 0, h)),
        ],
        out_specs=pl.BlockSpec((1, seq, RET_V_DIM), lambda b, h, *_: (b, 0, h)),
        scratch_shapes=[pltpu.VMEM((seq // RET_CHUNK, RET_QK_DIM, RET_V_DIM), BF16)] * 2,
    )
    return pl.pallas_call(
        _retention_kernel,
        grid_spec=grid_spec,
        out_shape=jax.ShapeDtypeStruct((batch, seq, RET_V_W), BF16),
        compiler_params=pltpu.CompilerParams(
            dimension_semantics=("arbitrary", "arbitrary"), vmem_limit_bytes=V7X_VMEM_LIMIT_BYTES),
        name="retention",
    )(dec_f, dec_b, q, kt, v, gs)


def _mix_out_mlp(y_ref, h_ref, wo_ref, gmlp_ref, w1_ref, w2_ref):
    h1 = h_ref[...] + _dot(y_ref[...], wo_ref[...])
    u = (h1 * gmlp_ref[...]).astype(BF16)
    r = _rms_scale(h1)
    mlp = None
    d_ff = w1_ref.shape[1]
    for c in range(d_ff // FF_CHUNK):
        lo = c * FF_CHUNK
        a = jnp.maximum(_dot(u, w1_ref[:, lo:lo + FF_CHUNK]), 0.0)
        part = _dot((a * a).astype(BF16), w2_ref[lo:lo + FF_CHUNK, :])
        mlp = part if mlp is None else mlp + part
    return h1 + (r * r) * mlp


def _tail_attn_proj_kernel(y_ref, h_ref, wo_ref, gmlp_ref, w1_ref, w2_ref,
                           gmix_ref, wqt_ref, wk_ref, wvt_ref, qn_ref, kn_ref,
                           cos2_ref, sin2_ref, cost_ref, sint_ref,
                           hout_ref, qt_ref, k_ref, vt_ref):
    h2 = _mix_out_mlp(y_ref, h_ref, wo_ref, gmlp_ref, w1_ref, w2_ref)
    hout_ref[...] = h2
    u = (h2 * gmix_ref[...]).astype(BF16)
    r = _rms_scale(h2)
    d = ATTN_HEAD_DIM
    half = d // 2
    r_t = jnp.broadcast_to(r, (h2.shape[0], d)).T[:1]

    cos2 = cos2_ref[...]
    sin2 = sin2_ref[...]
    k2 = _dot(u, wk_ref[...]) * r
    for hh in range(ATTN_KV_HEADS):
        kn = _rms_norm_rows(k2[:, hh * d:(hh + 1) * d], kn_ref[...])
        k_ref[:, hh * d:(hh + 1) * d] = (kn * cos2 + pltpu.roll(kn, half, 1) * sin2).astype(BF16)
    vt = _dot_nt(wvt_ref[...], u)
    for hh in range(ATTN_KV_HEADS):
        vt_ref[0, hh, :d, :] = (vt[hh * d:(hh + 1) * d] * r_t).astype(BF16)
        vt_ref[0, hh, d:, :] = jnp.ones((ATTN_VT_ROWS - d, vt.shape[1]), BF16)

    cos_t = cost_ref[...]
    sin_t = sint_ref[...]
    gain_t = qn_ref[...]
    for pair in range(ATTN_Q_HEADS // 2):
        q2 = _dot_nt(wqt_ref[pair * 2 * d:(pair + 1) * 2 * d, :], u)
        for hh in range(2):
            x = q2[hh * d:(hh + 1) * d]
            ms = jnp.mean(x * x, axis=0, keepdims=True) * (r_t * r_t)
            xn = x * (lax.rsqrt(ms + RMS_EPS) * r_t * ATTN_Q_SCALE) * gain_t
            x1 = xn[:half]
            x2 = xn[half:]
            lo = (pair * 2 + hh) * d
            qt_ref[0, lo:lo + half, :] = (x1 * cos_t - x2 * sin_t).astype(BF16)
            qt_ref[0, lo + half:lo + d, :] = (x1 * sin_t + x2 * cos_t).astype(BF16)


def _tail_final_kernel(y_ref, h_ref, wo_ref, gmlp_ref, w1_ref, w2_ref, gfin_ref, out_ref):
    h2 = _mix_out_mlp(y_ref, h_ref, wo_ref, gmlp_ref, w1_ref, w2_ref)
    out_ref[...] = _rms_norm_rows(h2, gfin_ref[...])


def _tail_common_specs(y, h, wo, w1, w2):
    tm = TOKEN_TILE
    d = h.shape[1]
    return [
        pl.BlockSpec((tm, y.shape[1]), lambda i: (i, 0)),
        pl.BlockSpec((tm, d), lambda i: (i, 0)),
        _resident(wo.shape),
        _resident((1, d)),
        _resident(w1.shape),
        _resident(w2.shape),
    ]


def _tail_attn_proj(y, h, wo, gmlp, w1, w2, gmix, wqt, wk, wvt, qn_t, kn, cos2, sin2, cos_t, sin_t,
                    batch, seq):
    tokens, d = h.shape
    tm = TOKEN_TILE
    tiles_per_seq = seq // tm
    half = ATTN_HEAD_DIM // 2
    pos_tile = pl.BlockSpec((tm, ATTN_HEAD_DIM), lambda i: (i % tiles_per_seq, 0))
    pos_tile_t = pl.BlockSpec((half, tm), lambda i: (0, i % tiles_per_seq))
    return pl.pallas_call(
        _tail_attn_proj_kernel,
        grid=(tokens // tm,),
        in_specs=_tail_common_specs(y, h, wo, w1, w2) + [
            _resident((1, d)),
            _resident(wqt.shape),
            _resident(wk.shape),
            _resident(wvt.shape),
            _resident(qn_t.shape),
            _resident((1, ATTN_HEAD_DIM)),
            pos_tile,
            pos_tile,
            pos_tile_t,
            pos_tile_t,
        ],
        out_specs=[
            pl.BlockSpec((tm, d), lambda i: (i, 0)),
            pl.BlockSpec((1, ATTN_Q_W, tm), lambda i: (i // tiles_per_seq, 0, i % tiles_per_seq)),
            pl.BlockSpec((tm, ATTN_KV_W), lambda i: (i, 0)),
            pl.BlockSpec((1, ATTN_KV_HEADS, ATTN_VT_ROWS, tm),
                         lambda i: (i // tiles_per_seq, 0, 0, i % tiles_per_seq)),
        ],
        out_shape=[
            jax.ShapeDtypeStruct((tokens, d), F32),
            jax.ShapeDtypeStruct((batch, ATTN_Q_W, seq), BF16),
            jax.ShapeDtypeStruct((tokens, ATTN_KV_W), BF16),
            jax.ShapeDtypeStruct((batch, ATTN_KV_HEADS, ATTN_VT_ROWS, seq), BF16),
        ],
        compiler_params=pltpu.CompilerParams(
            dimension_semantics=("arbitrary",), vmem_limit_bytes=V7X_VMEM_LIMIT_BYTES),
        name="tail_attn_proj",
    )(y, h, wo, gmlp, w1, w2, gmix, wqt, wk, wvt, qn_t, kn, cos2, sin2, cos_t, sin_t)


def _tail_final(y, h, wo, gmlp, w1, w2, gfin):
    tokens, d = h.shape
    tm = TOKEN_TILE
    return pl.pallas_call(
        _tail_final_kernel,
        grid=(tokens // tm,),
        in_specs=_tail_common_specs(y, h, wo, w1, w2) + [_resident((1, d))],
        out_specs=pl.BlockSpec((tm, d), lambda i: (i, 0)),
        out_shape=jax.ShapeDtypeStruct((tokens, d), F32),
        compiler_params=pltpu.CompilerParams(
            dimension_semantics=("arbitrary",), vmem_limit_bytes=V7X_VMEM_LIMIT_BYTES),
        name="tail_final",
    )(y, h, wo, gmlp, w1, w2, gfin)


def _attention_kernel(noshift_ref, qt_ref, k_ref, vt_ref, o_ref, s_scr, p_scr):
    d = ATTN_HEAD_DIM
    seq = k_ref.shape[1]
    tq = ATTN_Q_TILE
    kc = ATTN_KV_CHUNK
    units = [(j, g) for j in range(ATTN_Q_TILES_PER_STEP) for g in range(ATTN_GROUP)]

    def scores(u):
        j, g = units[u]
        qt = qt_ref[0, g * d:(g + 1) * d, j * tq:(j + 1) * tq]
        for c in range(seq // kc):
            s_scr[u % 2, c * kc:(c + 1) * kc, :] = _dot(k_ref[0, c * kc:(c + 1) * kc, :], qt)

    def softmax(u):
        m = None
        for c in range(seq // kc):
            mc = jnp.max(s_scr[u % 2, c * kc:(c + 1) * kc, :], axis=0, keepdims=True)
            m = mc if m is None else jnp.maximum(m, mc)
        for c in range(seq // kc):
            p = jnp.exp2(s_scr[u % 2, c * kc:(c + 1) * kc, :] - m)
            p_scr[u % 2, c * kc:(c + 1) * kc, :] = p.astype(BF16)

    def finish(u):
        j, g = units[u]
        acc = _dot(vt_ref[0, 0], p_scr[u % 2])
        o_ref[0, j * tq:(j + 1) * tq, g * d:(g + 1) * d] = (acc[:d] / acc[d:d + 1]).T.astype(BF16)

    def scores_exp(u):
        j, g = units[u]
        qt = qt_ref[0, g * d:(g + 1) * d, j * tq:(j + 1) * tq]
        l = None
        for c in range(seq // kc):
            p = jnp.exp2(_dot(k_ref[0, c * kc:(c + 1) * kc, :], qt))
            p_scr[u % 2, c * kc:(c + 1) * kc, :] = p.astype(BF16)
            lc = jnp.sum(p, axis=0, keepdims=True)
            l = lc if l is None else l + lc
        return l

    def finish_noshift(u, l):
        j, g = units[u]
        acc = _dot(vt_ref[0, 0, :d, :], p_scr[u % 2])
        o_ref[0, j * tq:(j + 1) * tq, g * d:(g + 1) * d] = (acc / l).T.astype(BF16)

    @pl.when(noshift_ref[0] == 1)
    def _():
        l_next = scores_exp(0)
        for u in range(len(units)):
            l = l_next
            if u + 1 < len(units):
                l_next = scores_exp(u + 1)
            finish_noshift(u, l)

    @pl.when(noshift_ref[0] != 1)
    def _():
        scores(0)
        for u in range(len(units)):
            if u + 1 < len(units):
                scores(u + 1)
            if u > 0:
                finish(u - 1)
            softmax(u)
        finish(len(units) - 1)


def _attention(noshift, qt, k, vt):
    batch, _, seq = qt.shape
    group_w = ATTN_GROUP * ATTN_HEAD_DIM
    tq_step = ATTN_Q_TILE * ATTN_Q_TILES_PER_STEP
    grid_spec = pltpu.PrefetchScalarGridSpec(
        num_scalar_prefetch=1,
        grid=(batch, ATTN_KV_HEADS, seq // tq_step),
        in_specs=[
            pl.BlockSpec((1, group_w, tq_step), lambda b, kk, t, *_: (b, kk, t)),
            pl.BlockSpec((1, seq, ATTN_HEAD_DIM), lambda b, kk, t, *_: (b, 0, kk)),
            pl.BlockSpec((1, 1, ATTN_VT_ROWS, seq), lambda b, kk, t, *_: (b, kk, 0, 0)),
        ],
        out_specs=pl.BlockSpec((1, tq_step, group_w), lambda b, kk, t, *_: (b, t, kk)),
        scratch_shapes=[pltpu.VMEM((2, seq, ATTN_Q_TILE), F32), pltpu.VMEM((2, seq, ATTN_Q_TILE), BF16)],
    )
    return pl.pallas_call(
        _attention_kernel,
        grid_spec=grid_spec,
        out_shape=jax.ShapeDtypeStruct((batch, seq, ATTN_Q_W), BF16),
        compiler_params=pltpu.CompilerParams(
            dimension_semantics=("arbitrary", "arbitrary", "arbitrary"),
            vmem_limit_bytes=V7X_VMEM_LIMIT_BYTES),
        name="attention",
    )(noshift, qt, k, vt)


def kernel(x, norm_mix, norm_mlp, mlp_w1, mlp_w2, ret_w_in, ret_w_out, ret_decay_fwd, ret_decay_bwd,
           attn_w_in, attn_w_out, attn_q_norm, attn_k_norm, final_norm):
    batch, seq, d = x.shape
    assert norm_mix.shape[0] == 2 and ret_w_in.shape[0] == 1 and attn_w_in.shape[0] == 1
    assert seq % TOKEN_TILE == 0 and seq % ATTN_Q_TILE == 0 and seq % GRID_W == 0
    tokens = batch * seq
    h = x.reshape(tokens, d)

    cos_r, sin_r = _axial_rope_tables(seq, RET_QK_DIM)
    cos_a, sin_a = _axial_rope_tables(seq, ATTN_HEAD_DIM)
    cos2 = jnp.concatenate([cos_a, cos_a], axis=-1)
    sin2 = jnp.concatenate([-sin_a, sin_a], axis=-1)

    w_in = ret_w_in[0]
    wq = w_in[:, :RET_QK_W].astype(BF16)
    wkt = w_in[:, RET_QK_W:2 * RET_QK_W].T.astype(BF16)
    wvg = w_in[:, 2 * RET_QK_W:].astype(BF16)
    q, kt, v, gs = _ret_proj(h, norm_mix[0][None, :], wq, wkt, wvg, cos_r, sin_r, batch, seq)
    y = _retention(q.reshape(batch, seq, RET_QK_W), kt, v.reshape(batch, seq, RET_V_W),
                   gs.reshape(batch, seq, RET_V_W), ret_decay_fwd[0], ret_decay_bwd[0])
    wa = attn_w_in[0]
    h, qta, ka, vta = _tail_attn_proj(
        y.reshape(tokens, RET_V_W), h, ret_w_out[0].astype(BF16), norm_mlp[0][None, :],
        mlp_w1[0].astype(BF16), mlp_w2[0].astype(BF16), norm_mix[1][None, :],
        wa[:, :ATTN_Q_W].T.astype(BF16), wa[:, ATTN_Q_W:ATTN_Q_W + ATTN_KV_W].astype(BF16),
        wa[:, ATTN_Q_W + ATTN_KV_W:].T.astype(BF16),
        jnp.broadcast_to(attn_q_norm[0][:, None], (ATTN_HEAD_DIM, TOKEN_TILE)), attn_k_norm[0][None, :],
        cos2, sin2, cos_a.T, sin_a.T, batch, seq)

    score_bound = (ATTN_HEAD_DIM * ATTN_Q_SCALE * 1.01) * (
        jnp.max(jnp.abs(attn_q_norm[0])) * jnp.max(jnp.abs(attn_k_norm[0])))
    noshift = (score_bound <= ATTN_NOSHIFT_MAX_LOG2).astype(jnp.int32).reshape(1)
    o = _attention(noshift, qta, ka.reshape(batch, seq, ATTN_KV_W), vta)
    out = _tail_final(o.reshape(tokens, d), h, attn_w_out[0].astype(BF16), norm_mlp[1][None, :],
                      mlp_w1[1].astype(BF16), mlp_w2[1].astype(BF16), final_norm[None, :])
    return out.reshape(batch, seq, d)
```

```python
import functools
import math

import jax
import jax.numpy as jnp
from jax import lax
from jax.experimental import pallas as pl
from jax.experimental.pallas import tpu as pltpu

BF16 = jnp.bfloat16
F32 = jnp.float32

GRID_W = 64
RMS_EPS = 1e-6
GN_EPS = 1e-6
ROPE_THETA = 10000.0

RET_HEADS = 4
RET_QK_DIM = 256
RET_V_DIM = 512
RET_QK_W = RET_HEADS * RET_QK_DIM
RET_V_W = RET_HEADS * RET_V_DIM
RET_CHUNK = 256
RET_HEADS_PER_STEP = 2

ATTN_HEAD_DIM = 128
ATTN_Q_HEADS = 8
ATTN_KV_HEADS = 2
ATTN_GROUP = ATTN_Q_HEADS // ATTN_KV_HEADS
ATTN_Q_W = ATTN_Q_HEADS * ATTN_HEAD_DIM
ATTN_KV_W = ATTN_KV_HEADS * ATTN_HEAD_DIM
ATTN_IN_W = ATTN_Q_W + 2 * ATTN_KV_W

TOKEN_TILE = 512
WIDE_TOKEN_TILE = 1024
ATTN_Q_TILE = 256
ATTN_Q_TILES_PER_STEP = 4
ATTN_KV_CHUNK = 256
ATTN_VT_ROWS = ATTN_HEAD_DIM + 16
ATTN_Q_SCALE = ATTN_HEAD_DIM ** -0.5 * math.log2(math.e)
ATTN_NOSHIFT_MAX_LOG2 = 64.0
FF_CHUNK = 1024
V7X_VMEM_LIMIT_BYTES = 56 * 1024 * 1024


def _dot(a, b):
    return jnp.dot(a, b, preferred_element_type=F32)


def _dot_nt(a, b):
    return lax.dot_general(a, b, (((1,), (1,)), ((), ())), preferred_element_type=F32)


def _rms_scale(x):
    return lax.rsqrt(jnp.mean(x * x, axis=-1, keepdims=True) + RMS_EPS)


def _rms_norm_rows(x, gain):
    return x * _rms_scale(x) * gain


def _resident(shape):
    nd = len(shape)
    return pl.BlockSpec(shape, lambda *_: (0,) * nd, pipeline_mode=pl.Buffered(1))


def _axial_rope_tables(seq, rot_dim):
    rows = seq // GRID_W
    row = jnp.broadcast_to(jnp.arange(rows, dtype=F32)[:, None], (rows, GRID_W)).reshape(seq)
    col = jnp.broadcast_to(jnp.arange(GRID_W, dtype=F32)[None, :], (rows, GRID_W)).reshape(seq)
    per_axis = rot_dim // 2
    n_freq = per_axis // 2
    inv_freq = ROPE_THETA ** (-jnp.arange(n_freq, dtype=F32) * 2.0 / per_axis)
    ang = jnp.concatenate([row[:, None] * inv_freq[None, :], col[:, None] * inv_freq[None, :]], axis=-1)
    return jnp.cos(ang), jnp.sin(ang)


def _ret_proj_kernel(h_ref, gain_ref, wq_ref, wkt_ref, wvg_ref, cos_ref, sin_ref, cost_ref, sint_ref,
                     q_ref, kt_ref, v_ref, gs_ref):
    x = h_ref[...]
    u = (x * gain_ref[...]).astype(BF16)
    r = _rms_scale(x)
    half = RET_QK_DIM // 2
    r_t = jnp.broadcast_to(r, (x.shape[0], half)).T
    for c in range(RET_HEADS):
        lo = c * RET_V_DIM
        g = _dot(u, wvg_ref[:, RET_V_W + lo:RET_V_W + lo + RET_V_DIM]) * r
        gs_ref[:, lo:lo + RET_V_DIM] = (g * jax.nn.sigmoid(g)).astype(BF16)
    k_scale = RET_QK_DIM ** -0.5
    cos_t = cost_ref[...] * (r_t * k_scale)
    sin_t = sint_ref[...] * (r_t * k_scale)
    for hh in range(RET_HEADS):
        lo = hh * RET_QK_DIM
        kh = _dot_nt(wkt_ref[lo:lo + RET_QK_DIM, :], u)
        x1 = kh[:half]
        x2 = kh[half:]
        kt_ref[0, lo:lo + half, :] = (x1 * cos_t - x2 * sin_t).astype(BF16)
        kt_ref[0, lo + half:lo + RET_QK_DIM, :] = (x1 * sin_t + x2 * cos_t).astype(BF16)
    cos = cos_ref[...] * r
    sin = sin_ref[...] * r
    for hh in range(RET_HEADS):
        lo = hh * RET_QK_DIM
        qh = _dot(u, wq_ref[:, lo:lo + RET_QK_DIM])
        x1 = qh[:, :half]
        x2 = qh[:, half:]
        q_ref[:, lo:lo + half] = (x1 * cos - x2 * sin).astype(BF16)
        q_ref[:, lo + half:lo + RET_QK_DIM] = (x1 * sin + x2 * cos).astype(BF16)
    for c in range(RET_HEADS):
        lo = c * RET_V_DIM
        v_ref[:, lo:lo + RET_V_DIM] = (_dot(u, wvg_ref[:, lo:lo + RET_V_DIM]) * r).astype(BF16)


def _ret_proj(h, gain, wq, wkt, wvg, cos, sin, batch, seq):
    tokens, d = h.shape
    tm = WIDE_TOKEN_TILE
    tiles_per_seq = seq // tm
    half = RET_QK_DIM // 2
    cos_t = cos.T
    sin_t = sin.T
    row_tile = lambda width: pl.BlockSpec((tm, width), lambda i: (i, 0))
    return pl.pallas_call(
        _ret_proj_kernel,
        grid=(tokens // tm,),
        in_specs=[
            row_tile(d),
            _resident((1, d)),
            _resident(wq.shape),
            _resident(wkt.shape),
            _resident(wvg.shape),
            pl.BlockSpec((tm, half), lambda i: (i % tiles_per_seq, 0)),
            pl.BlockSpec((tm, half), lambda i: (i % tiles_per_seq, 0)),
            pl.BlockSpec((half, tm), lambda i: (0, i % tiles_per_seq)),
            pl.BlockSpec((half, tm), lambda i: (0, i % tiles_per_seq)),
        ],
        out_specs=[
            row_tile(RET_QK_W),
            pl.BlockSpec((1, RET_QK_W, tm), lambda i: (i // tiles_per_seq, 0, i % tiles_per_seq)),
            row_tile(RET_V_W),
            row_tile(RET_V_W),
        ],
        out_shape=[
            jax.ShapeDtypeStruct((tokens, RET_QK_W), BF16),
            jax.ShapeDtypeStruct((batch, RET_QK_W, seq), BF16),
            jax.ShapeDtypeStruct((tokens, RET_V_W), BF16),
            jax.ShapeDtypeStruct((tokens, RET_V_W), BF16),
        ],
        compiler_params=pltpu.CompilerParams(
            dimension_semantics=("arbitrary",), vmem_limit_bytes=V7X_VMEM_LIMIT_BYTES),
        name="ret_proj",
    )(h, gain, wq, wkt, wvg, cos, sin, cos_t, sin_t)


def _retention_kernel(dec_f_ref, dec_b_ref, q_ref, kt_ref, v_ref, gs_ref, y_ref, sf_scr, sb_scr):
    c = RET_CHUNK
    dk = RET_QK_DIM
    dv = RET_V_DIM
    n_chunks = q_ref.shape[1] // c
    row = lax.broadcasted_iota(jnp.int32, (c, c), 0).astype(F32)
    col = lax.broadcasted_iota(jnp.int32, (c, c), 1).astype(F32)
    diff = row - col

    def rows(i):
        return slice(i * c, (i + 1) * c)

    for hh in range(RET_HEADS_PER_STEP):
        head = pl.program_id(1) * RET_HEADS_PER_STEP + hh
        qk_cols = slice(hh * dk, (hh + 1) * dk)
        v_cols = slice(hh * dv, (hh + 1) * dv)
        lgf = -jnp.exp(jnp.full((c, c), dec_f_ref[head], F32))
        lgb = -jnp.exp(jnp.full((c, c), dec_b_ref[head], F32))
        decay = jnp.where(diff >= 0, jnp.exp(lgf * jnp.maximum(diff, 0.0)),
                          jnp.exp(lgb * jnp.maximum(-diff, 0.0)))
        xi_f = jnp.exp(lgf * (row + 1.0)).astype(BF16)
        xi_b = jnp.exp(lgb * (float(c) - row)).astype(BF16)
        zeta_f = jnp.exp(lgf * (float(c) - 1.0 - col)).astype(BF16)
        zeta_b = jnp.exp(lgb * col).astype(BF16)
        gc_f = jnp.exp(-jnp.exp(jnp.full((dk, dv), dec_f_ref[head], F32)) * float(c))
        gc_b = jnp.exp(-jnp.exp(jnp.full((dk, dv), dec_b_ref[head], F32)) * float(c))

        state = None
        for i in range(n_chunks - 1):
            upd = _dot(kt_ref[0, qk_cols, rows(i)] * zeta_f, v_ref[0, rows(i), v_cols])
            state = upd if state is None else state * gc_f + upd
            sf_scr[hh, i + 1] = state.astype(BF16)
        state = None
        for i in range(n_chunks - 1, 0, -1):
            upd = _dot(kt_ref[0, qk_cols, rows(i)] * zeta_b, v_ref[0, rows(i), v_cols])
            state = upd if state is None else state * gc_b + upd
            sb_scr[hh, i - 1] = state.astype(BF16)

        for i in range(n_chunks):
            qi = q_ref[0, rows(i), qk_cols]
            scores = _dot(qi, kt_ref[0, qk_cols, rows(i)]) * decay
            y = _dot(scores.astype(BF16), v_ref[0, rows(i), v_cols])
            if i > 0:
                y = y + _dot(qi * xi_f, sf_scr[hh, i])
            if i < n_chunks - 1:
                y = y + _dot(qi * xi_b, sb_scr[hh, i])
            mu = jnp.mean(y, axis=-1, keepdims=True)
            yc = y - mu
            var = jnp.mean(yc * yc, axis=-1, keepdims=True)
            yn = (yc * lax.rsqrt(var + GN_EPS)).astype(BF16)
            y_ref[0, rows(i), v_cols] = gs_ref[0, rows(i), v_cols] * yn


def _retention(q, kt, v, gs, dec_f, dec_b):
    batch, seq, _ = q.shape
    assert RET_CHUNK == RET_QK_DIM and seq % RET_CHUNK == 0
    hps = RET_HEADS_PER_STEP
    grid_spec = pltpu.PrefetchScalarGridSpec(
        num_scalar_prefetch=2,
        grid=(batch, RET_HEADS // hps),
        in_specs=[
            pl.BlockSpec((1, seq, hps * RET_QK_DIM), lambda b, h, *_: (b, 0, h)),
            pl.BlockSpec((1, hps * RET_QK_DIM, seq), lambda b, h, *_: (b, h, 0)),
            pl.BlockSpec((1, seq, hps * RET_V_DIM), lambda b, h, *_: (b, 0, h)),
            pl.BlockSpec((1, seq, hps * RET_V_DIM), lambda b, h, *_: (b, 0, h)),
        ],
        out_specs=pl.BlockSpec((1, seq, hps * RET_V_DIM), lambda b, h, *_: (b, 0, h)),
        scratch_shapes=[pltpu.VMEM((hps, seq // RET_CHUNK, RET_QK_DIM, RET_V_DIM), BF16)] * 2,
    )
    return pl.pallas_call(
        _retention_kernel,
        grid_spec=grid_spec,
        out_shape=jax.ShapeDtypeStruct((batch, seq, RET_V_W), BF16),
        compiler_params=pltpu.CompilerParams(
            dimension_semantics=("arbitrary", "arbitrary"), vmem_limit_bytes=V7X_VMEM_LIMIT_BYTES),
        name="retention",
    )(dec_f, dec_b, q, kt, v, gs)


def _mix_out_mlp(y_ref, h_ref, wo_ref, gmlp_ref, w1_ref, w2_ref):
    h1 = h_ref[...] + _dot(y_ref[...], wo_ref[...])
    u = (h1 * gmlp_ref[...]).astype(BF16)
    r = _rms_scale(h1)
    mlp = None
    d_ff = w1_ref.shape[1]
    for c in range(d_ff // FF_CHUNK):
        lo = c * FF_CHUNK
        a = jnp.maximum(_dot(u, w1_ref[:, lo:lo + FF_CHUNK]), 0.0)
        part = _dot((a * a).astype(BF16), w2_ref[lo:lo + FF_CHUNK, :])
        mlp = part if mlp is None else mlp + part
    return h1 + (r * r) * mlp


def _tail_attn_proj_kernel(y_ref, h_ref, wo_ref, gmlp_ref, w1_ref, w2_ref,
                           gmix_ref, wqt_ref, wk_ref, wvt_ref, qn_ref, kn_ref,
                           cos2_ref, sin2_ref, cost_ref, sint_ref,
                           hout_ref, qt_ref, k_ref, vt_ref):
    h2 = _mix_out_mlp(y_ref, h_ref, wo_ref, gmlp_ref, w1_ref, w2_ref)
    hout_ref[...] = h2
    u = (h2 * gmix_ref[...]).astype(BF16)
    r = _rms_scale(h2)
    d = ATTN_HEAD_DIM
    half = d // 2
    r_t = jnp.broadcast_to(r, (h2.shape[0], d)).T[:1]

    cos2 = cos2_ref[...]
    sin2 = sin2_ref[...]
    k2 = _dot(u, wk_ref[...]) * r
    for hh in range(ATTN_KV_HEADS):
        kn = _rms_norm_rows(k2[:, hh * d:(hh + 1) * d], kn_ref[...])
        k_ref[:, hh * d:(hh + 1) * d] = (kn * cos2 + pltpu.roll(kn, half, 1) * sin2).astype(BF16)
    vt = _dot_nt(wvt_ref[...], u)
    for hh in range(ATTN_KV_HEADS):
        vt_ref[0, hh, :d, :] = (vt[hh * d:(hh + 1) * d] * r_t).astype(BF16)
        vt_ref[0, hh, d:, :] = jnp.ones((ATTN_VT_ROWS - d, vt.shape[1]), BF16)

    cos_t = cost_ref[...]
    sin_t = sint_ref[...]
    gain_t = qn_ref[...]
    for pair in range(ATTN_Q_HEADS // 2):
        q2 = _dot_nt(wqt_ref[pair * 2 * d:(pair + 1) * 2 * d, :], u)
        for hh in range(2):
            x = q2[hh * d:(hh + 1) * d]
            ms = jnp.mean(x * x, axis=0, keepdims=True) * (r_t * r_t)
            xn = x * (lax.rsqrt(ms + RMS_EPS) * r_t * ATTN_Q_SCALE) * gain_t
            x1 = xn[:half]
            x2 = xn[half:]
            lo = (pair * 2 + hh) * d
            qt_ref[0, lo:lo + half, :] = (x1 * cos_t - x2 * sin_t).astype(BF16)
            qt_ref[0, lo + half:lo + d, :] = (x1 * sin_t + x2 * cos_t).astype(BF16)


def _tail_final_kernel(y_ref, h_ref, wo_ref, gmlp_ref, w1_ref, w2_ref, gfin_ref, out_ref):
    h2 = _mix_out_mlp(y_ref, h_ref, wo_ref, gmlp_ref, w1_ref, w2_ref)
    out_ref[...] = _rms_norm_rows(h2, gfin_ref[...])


def _tail_common_specs(y, h, wo, w1, w2, tm):
    d = h.shape[1]
    return [
        pl.BlockSpec((tm, y.shape[1]), lambda i: (i, 0)),
        pl.BlockSpec((tm, d), lambda i: (i, 0)),
        _resident(wo.shape),
        _resident((1, d)),
        _resident(w1.shape),
        _resident(w2.shape),
    ]


def _tail_attn_proj(y, h, wo, gmlp, w1, w2, gmix, wqt, wk, wvt, qn_t, kn, cos2, sin2, cos_t, sin_t,
                    batch, seq):
    tokens, d = h.shape
    tm = TOKEN_TILE
    tiles_per_seq = seq // tm
    half = ATTN_HEAD_DIM // 2
    pos_tile = pl.BlockSpec((tm, ATTN_HEAD_DIM), lambda i: (i % tiles_per_seq, 0))
    pos_tile_t = pl.BlockSpec((half, tm), lambda i: (0, i % tiles_per_seq))
    return pl.pallas_call(
        _tail_attn_proj_kernel,
        grid=(tokens // tm,),
        in_specs=_tail_common_specs(y, h, wo, w1, w2, tm) + [
            _resident((1, d)),
            _resident(wqt.shape),
            _resident(wk.shape),
            _resident(wvt.shape),
            _resident(qn_t.shape),
            _resident((1, ATTN_HEAD_DIM)),
            pos_tile,
            pos_tile,
            pos_tile_t,
            pos_tile_t,
        ],
        out_specs=[
            pl.BlockSpec((tm, d), lambda i: (i, 0)),
            pl.BlockSpec((1, ATTN_Q_W, tm), lambda i: (i // tiles_per_seq, 0, i % tiles_per_seq)),
            pl.BlockSpec((tm, ATTN_KV_W), lambda i: (i, 0)),
            pl.BlockSpec((1, ATTN_KV_HEADS, ATTN_VT_ROWS, tm),
                         lambda i: (i // tiles_per_seq, 0, 0, i % tiles_per_seq)),
        ],
        out_shape=[
            jax.ShapeDtypeStruct((tokens, d), F32),
            jax.ShapeDtypeStruct((batch, ATTN_Q_W, seq), BF16),
            jax.ShapeDtypeStruct((tokens, ATTN_KV_W), BF16),
            jax.ShapeDtypeStruct((batch, ATTN_KV_HEADS, ATTN_VT_ROWS, seq), BF16),
        ],
        compiler_params=pltpu.CompilerParams(
            dimension_semantics=("arbitrary",), vmem_limit_bytes=V7X_VMEM_LIMIT_BYTES),
        name="tail_attn_proj",
    )(y, h, wo, gmlp, w1, w2, gmix, wqt, wk, wvt, qn_t, kn, cos2, sin2, cos_t, sin_t)


def _tail_final(y, h, wo, gmlp, w1, w2, gfin):
    tokens, d = h.shape
    tm = WIDE_TOKEN_TILE
    return pl.pallas_call(
        _tail_final_kernel,
        grid=(tokens // tm,),
        in_specs=_tail_common_specs(y, h, wo, w1, w2, tm) + [_resident((1, d))],
        out_specs=pl.BlockSpec((tm, d), lambda i: (i, 0)),
        out_shape=jax.ShapeDtypeStruct((tokens, d), F32),
        compiler_params=pltpu.CompilerParams(
            dimension_semantics=("arbitrary",), vmem_limit_bytes=V7X_VMEM_LIMIT_BYTES),
        name="tail_final",
    )(y, h, wo, gmlp, w1, w2, gfin)


def _attention_kernel(noshift_ref, qt_ref, k_ref, vt_ref, o_ref, s_scr, p_scr):
    d = ATTN_HEAD_DIM
    seq = k_ref.shape[1]
    tq = ATTN_Q_TILE
    kc = ATTN_KV_CHUNK
    units = [(j, g) for j in range(ATTN_Q_TILES_PER_STEP) for g in range(ATTN_GROUP)]

    def scores(u):
        j, g = units[u]
        qt = qt_ref[0, g * d:(g + 1) * d, j * tq:(j + 1) * tq]
        for c in range(seq // kc):
            s_scr[u % 2, c * kc:(c + 1) * kc, :] = _dot(k_ref[0, c * kc:(c + 1) * kc, :], qt)

    def softmax(u):
        m = None
        for c in range(seq // kc):
            mc = jnp.max(s_scr[u % 2, c * kc:(c + 1) * kc, :], axis=0, keepdims=True)
            m = mc if m is None else jnp.maximum(m, mc)
        for c in range(seq // kc):
            p = jnp.exp2(s_scr[u % 2, c * kc:(c + 1) * kc, :] - m)
            p_scr[u % 2, c * kc:(c + 1) * kc, :] = p.astype(BF16)

    def finish(u):
        j, g = units[u]
        acc = _dot(vt_ref[0, 0], p_scr[u % 2])
        o_ref[0, j * tq:(j + 1) * tq, g * d:(g + 1) * d] = (acc[:d] / acc[d:d + 1]).T.astype(BF16)

    def scores_exp(u):
        j, g = units[u]
        qt = qt_ref[0, g * d:(g + 1) * d, j * tq:(j + 1) * tq]
        l = None
        for c in range(seq // kc):
            p = jnp.exp2(_dot(k_ref[0, c * kc:(c + 1) * kc, :], qt))
            p_scr[u % 2, c * kc:(c + 1) * kc, :] = p.astype(BF16)
            lc = jnp.sum(p, axis=0, keepdims=True)
            l = lc if l is None else l + lc
        return l

    def finish_noshift(u, l):
        j, g = units[u]
        acc = _dot(vt_ref[0, 0, :d, :], p_scr[u % 2])
        o_ref[0, j * tq:(j + 1) * tq, g * d:(g + 1) * d] = (acc / l).T.astype(BF16)

    @pl.when(noshift_ref[0] == 1)
    def _():
        l_next = scores_exp(0)
        for u in range(len(units)):
            l = l_next
            if u + 1 < len(units):
                l_next = scores_exp(u + 1)
            finish_noshift(u, l)

    @pl.when(noshift_ref[0] != 1)
    def _():
        scores(0)
        for u in range(len(units)):
            if u + 1 < len(units):
                scores(u + 1)
            if u > 0:
                finish(u - 1)
            softmax(u)
        finish(len(units) - 1)


def _attention(noshift, qt, k, vt):
    batch, _, seq = qt.shape
    group_w = ATTN_GROUP * ATTN_HEAD_DIM
    tq_step = ATTN_Q_TILE * ATTN_Q_TILES_PER_STEP
    grid_spec = pltpu.PrefetchScalarGridSpec(
        num_scalar_prefetch=1,
        grid=(batch, ATTN_KV_HEADS, seq // tq_step),
        in_specs=[
            pl.BlockSpec((1, group_w, tq_step), lambda b, kk, t, *_: (b, kk, t)),
            pl.BlockSpec((1, seq, ATTN_HEAD_DIM), lambda b, kk, t, *_: (b, 0, kk)),
            pl.BlockSpec((1, 1, ATTN_VT_ROWS, seq), lambda b, kk, t, *_: (b, kk, 0, 0)),
        ],
        out_specs=pl.BlockSpec((1, tq_step, group_w), lambda b, kk, t, *_: (b, t, kk)),
        scratch_shapes=[pltpu.VMEM((2, seq, ATTN_Q_TILE), F32), pltpu.VMEM((2, seq, ATTN_Q_TILE), BF16)],
    )
    return pl.pallas_call(
        _attention_kernel,
        grid_spec=grid_spec,
        out_shape=jax.ShapeDtypeStruct((batch, seq, ATTN_Q_W), BF16),
        compiler_params=pltpu.CompilerParams(
            dimension_semantics=("arbitrary", "arbitrary", "arbitrary"),
            vmem_limit_bytes=V7X_VMEM_LIMIT_BYTES),
        name="attention",
    )(noshift, qt, k, vt)


def kernel(x, norm_mix, norm_mlp, mlp_w1, mlp_w2, ret_w_in, ret_w_out, ret_decay_fwd, ret_decay_bwd,
           attn_w_in, attn_w_out, attn_q_norm, attn_k_norm, final_norm):
    batch, seq, d = x.shape
    assert norm_mix.shape[0] == 2 and ret_w_in.shape[0] == 1 and attn_w_in.shape[0] == 1
    assert seq % WIDE_TOKEN_TILE == 0 and seq % TOKEN_TILE == 0 and seq % GRID_W == 0
    tokens = batch * seq
    h = x.reshape(tokens, d)

    cos_r, sin_r = _axial_rope_tables(seq, RET_QK_DIM)
    cos_a, sin_a = _axial_rope_tables(seq, ATTN_HEAD_DIM)
    cos2 = jnp.concatenate([cos_a, cos_a], axis=-1)
    sin2 = jnp.concatenate([-sin_a, sin_a], axis=-1)

    w_in = ret_w_in[0]
    wq = w_in[:, :RET_QK_W].astype(BF16)
    wkt = w_in[:, RET_QK_W:2 * RET_QK_W].T.astype(BF16)
    wvg = w_in[:, 2 * RET_QK_W:].astype(BF16)
    q, kt, v, gs = _ret_proj(h, norm_mix[0][None, :], wq, wkt, wvg, cos_r, sin_r, batch, seq)
    y = _retention(q.reshape(batch, seq, RET_QK_W), kt, v.reshape(batch, seq, RET_V_W),
                   gs.reshape(batch, seq, RET_V_W), ret_decay_fwd[0], ret_decay_bwd[0])
    wa = attn_w_in[0]
    h, qta, ka, vta = _tail_attn_proj(
        y.reshape(tokens, RET_V_W), h, ret_w_out[0].astype(BF16), norm_mlp[0][None, :],
        mlp_w1[0].astype(BF16), mlp_w2[0].astype(BF16), norm_mix[1][None, :],
        wa[:, :ATTN_Q_W].T.astype(BF16), wa[:, ATTN_Q_W:ATTN_Q_W + ATTN_KV_W].astype(BF16),
        wa[:, ATTN_Q_W + ATTN_KV_W:].T.astype(BF16),
        jnp.broadcast_to(attn_q_norm[0][:, None], (ATTN_HEAD_DIM, TOKEN_TILE)), attn_k_norm[0][None, :],
        cos2, sin2, cos_a.T, sin_a.T, batch, seq)

    score_bound = (ATTN_HEAD_DIM * ATTN_Q_SCALE * 1.01) * (
        jnp.max(jnp.abs(attn_q_norm[0])) * jnp.max(jnp.abs(attn_k_norm[0])))
    noshift = (score_bound <= ATTN_NOSHIFT_MAX_LOG2).astype(jnp.int32).reshape(1)
    o = _attention(noshift, qta, ka.reshape(batch, seq, ATTN_KV_W), vta)
    out = _tail_final(o.reshape(tokens, d), h, attn_w_out[0].astype(BF16), norm_mlp[1][None, :],
                      mlp_w1[1].astype(BF16), mlp_w2[1].astype(BF16), final_norm[None, :])
    return out.reshape(batch, seq, d)
```

```python
import math

import jax
import jax.numpy as jnp
import numpy as np
from jax import lax
from jax.experimental import pallas as pl
from jax.experimental.pallas import tpu as pltpu

BF16 = jnp.bfloat16
F32 = jnp.float32

GRID_W = 64
RMS_EPS = 1e-6
GN_EPS = 1e-6
ROPE_THETA = 10000.0

RET_HEADS = 4
RET_QK_DIM = 256
RET_V_DIM = 512
RET_QK_W = RET_HEADS * RET_QK_DIM
RET_V_W = RET_HEADS * RET_V_DIM
RET_CHUNK = 256
RET_HEADS_PER_STEP = 2

ATTN_HEAD_DIM = 128
ATTN_Q_HEADS = 8
ATTN_KV_HEADS = 2
ATTN_GROUP = ATTN_Q_HEADS // ATTN_KV_HEADS
ATTN_Q_W = ATTN_Q_HEADS * ATTN_HEAD_DIM
ATTN_KV_W = ATTN_KV_HEADS * ATTN_HEAD_DIM

TOKEN_TILE = 512
WIDE_TOKEN_TILE = 1024
ATTN_Q_TILE = 256
ATTN_Q_TILES_PER_STEP = 4
ATTN_KV_CHUNK = 256
ATTN_VT_ROWS = ATTN_HEAD_DIM + 16
ATTN_Q_SCALE = ATTN_HEAD_DIM ** -0.5 * math.log2(math.e)
ATTN_NOSHIFT_MAX_LOG2 = 64.0
FF_CHUNK = 1024
V7X_VMEM_LIMIT_BYTES = 56 * 1024 * 1024


def _dot(a, b):
    return jnp.dot(a, b, preferred_element_type=F32)


def _dot_nt(a, b):
    return lax.dot_general(a, b, (((1,), (1,)), ((), ())), preferred_element_type=F32)


def _rms_scale(x):
    return lax.rsqrt(jnp.mean(x * x, axis=-1, keepdims=True) + RMS_EPS)


def _rms_norm_rows(x, gain):
    return x * _rms_scale(x) * gain


def _resident(shape):
    nd = len(shape)
    return pl.BlockSpec(shape, lambda *_: (0,) * nd, pipeline_mode=pl.Buffered(1))


def _axial_rope_tables(seq, rot_dim):
    rows = seq // GRID_W
    row = np.repeat(np.arange(rows, dtype=np.float64), GRID_W)
    col = np.tile(np.arange(GRID_W, dtype=np.float64), rows)
    per_axis = rot_dim // 2
    n_freq = per_axis // 2
    inv_freq = ROPE_THETA ** (-np.arange(n_freq, dtype=np.float64) * 2.0 / per_axis)
    ang = np.concatenate([row[:, None] * inv_freq[None, :], col[:, None] * inv_freq[None, :]], axis=-1)
    return np.cos(ang).astype(np.float32), np.sin(ang).astype(np.float32)


def _ret_proj_kernel(h_ref, gain_ref, w_ref, cos_ref, sin_ref, cost_ref, sint_ref,
                     q_ref, kt_ref, v_ref, gs_ref, wkt_scr):
    k_col0, v_col0, g_col0 = RET_QK_W, 2 * RET_QK_W, 2 * RET_QK_W + RET_V_W

    @pl.when(pl.program_id(0) == 0)
    def _():
        for hh in range(RET_HEADS):
            lo = hh * RET_QK_DIM
            wkt_scr[lo:lo + RET_QK_DIM, :] = w_ref[:, k_col0 + lo:k_col0 + lo + RET_QK_DIM].T

    x = h_ref[...]
    u = (x * gain_ref[...]).astype(BF16)
    r = _rms_scale(x)
    half = RET_QK_DIM // 2
    r_t = jnp.broadcast_to(r, (x.shape[0], half)).T
    for c in range(RET_HEADS):
        lo = c * RET_V_DIM
        g = _dot(u, w_ref[:, g_col0 + lo:g_col0 + lo + RET_V_DIM]) * r
        gs_ref[:, lo:lo + RET_V_DIM] = (g * jax.nn.sigmoid(g)).astype(BF16)
    k_scale = RET_QK_DIM ** -0.5
    cos_t = cost_ref[...] * (r_t * k_scale)
    sin_t = sint_ref[...] * (r_t * k_scale)
    for hh in range(RET_HEADS):
        lo = hh * RET_QK_DIM
        kh = _dot_nt(wkt_scr[lo:lo + RET_QK_DIM, :], u)
        x1 = kh[:half]
        x2 = kh[half:]
        kt_ref[0, lo:lo + half, :] = (x1 * cos_t - x2 * sin_t).astype(BF16)
        kt_ref[0, lo + half:lo + RET_QK_DIM, :] = (x1 * sin_t + x2 * cos_t).astype(BF16)
    cos = cos_ref[...] * r
    sin = sin_ref[...] * r
    for hh in range(RET_HEADS):
        lo = hh * RET_QK_DIM
        qh = _dot(u, w_ref[:, lo:lo + RET_QK_DIM])
        x1 = qh[:, :half]
        x2 = qh[:, half:]
        q_ref[:, lo:lo + half] = (x1 * cos - x2 * sin).astype(BF16)
        q_ref[:, lo + half:lo + RET_QK_DIM] = (x1 * sin + x2 * cos).astype(BF16)
    for c in range(RET_HEADS):
        lo = c * RET_V_DIM
        v_ref[:, lo:lo + RET_V_DIM] = (_dot(u, w_ref[:, v_col0 + lo:v_col0 + lo + RET_V_DIM]) * r).astype(BF16)


def _ret_proj(h, gain, w, cos_tabs, batch, seq):
    cos, sin, cos_t, sin_t = cos_tabs
    tokens, d = h.shape
    tm = WIDE_TOKEN_TILE
    tiles_per_seq = seq // tm
    half = RET_QK_DIM // 2
    row_tile = lambda width: pl.BlockSpec((tm, width), lambda i: (i, 0))
    return pl.pallas_call(
        _ret_proj_kernel,
        grid=(tokens // tm,),
        in_specs=[
            row_tile(d),
            _resident((1, d)),
            _resident(w.shape),
            pl.BlockSpec((tm, half), lambda i: (i % tiles_per_seq, 0)),
            pl.BlockSpec((tm, half), lambda i: (i % tiles_per_seq, 0)),
            pl.BlockSpec((half, tm), lambda i: (0, i % tiles_per_seq)),
            pl.BlockSpec((half, tm), lambda i: (0, i % tiles_per_seq)),
        ],
        out_specs=[
            row_tile(RET_QK_W),
            pl.BlockSpec((1, RET_QK_W, tm), lambda i: (i // tiles_per_seq, 0, i % tiles_per_seq)),
            row_tile(RET_V_W),
            row_tile(RET_V_W),
        ],
        out_shape=[
            jax.ShapeDtypeStruct((tokens, RET_QK_W), BF16),
            jax.ShapeDtypeStruct((batch, RET_QK_W, seq), BF16),
            jax.ShapeDtypeStruct((tokens, RET_V_W), BF16),
            jax.ShapeDtypeStruct((tokens, RET_V_W), BF16),
        ],
        scratch_shapes=[pltpu.VMEM((RET_QK_W, d), BF16)],
        compiler_params=pltpu.CompilerParams(
            dimension_semantics=("arbitrary",), vmem_limit_bytes=V7X_VMEM_LIMIT_BYTES),
        name="ret_proj",
    )(h, gain, w, cos, sin, cos_t, sin_t)


def _retention_kernel(dec_f_ref, dec_b_ref, q_ref, kt_ref, v_ref, gs_ref, y_ref, sf_scr, sb_scr):
    c = RET_CHUNK
    dk = RET_QK_DIM
    dv = RET_V_DIM
    n_chunks = q_ref.shape[1] // c
    row = lax.broadcasted_iota(jnp.int32, (c, c), 0).astype(F32)
    col = lax.broadcasted_iota(jnp.int32, (c, c), 1).astype(F32)
    diff = row - col

    def rows(i):
        return slice(i * c, (i + 1) * c)

    for hh in range(RET_HEADS_PER_STEP):
        head = pl.program_id(1) * RET_HEADS_PER_STEP + hh
        qk_cols = slice(hh * dk, (hh + 1) * dk)
        v_cols = slice(hh * dv, (hh + 1) * dv)
        lgf = -jnp.exp(jnp.full((c, c), dec_f_ref[head], F32))
        lgb = -jnp.exp(jnp.full((c, c), dec_b_ref[head], F32))
        decay = jnp.where(diff >= 0, jnp.exp(lgf * jnp.maximum(diff, 0.0)),
                          jnp.exp(lgb * jnp.maximum(-diff, 0.0)))
        xi_f = jnp.exp(lgf * (row + 1.0)).astype(BF16)
        xi_b = jnp.exp(lgb * (float(c) - row)).astype(BF16)
        zeta_f = jnp.exp(lgf * (float(c) - 1.0 - col)).astype(BF16)
        zeta_b = jnp.exp(lgb * col).astype(BF16)
        gc_f = jnp.exp(-jnp.exp(jnp.full((dk, dv), dec_f_ref[head], F32)) * float(c))
        gc_b = jnp.exp(-jnp.exp(jnp.full((dk, dv), dec_b_ref[head], F32)) * float(c))

        state = None
        for i in range(n_chunks - 1):
            upd = _dot(kt_ref[0, qk_cols, rows(i)] * zeta_f, v_ref[0, rows(i), v_cols])
            state = upd if state is None else state * gc_f + upd
            sf_scr[hh, i + 1] = state.astype(BF16)
        state = None
        for i in range(n_chunks - 1, 0, -1):
            upd = _dot(kt_ref[0, qk_cols, rows(i)] * zeta_b, v_ref[0, rows(i), v_cols])
            state = upd if state is None else state * gc_b + upd
            sb_scr[hh, i - 1] = state.astype(BF16)

        for i in range(n_chunks):
            qi = q_ref[0, rows(i), qk_cols]
            scores = _dot(qi, kt_ref[0, qk_cols, rows(i)]) * decay
            y = _dot(scores.astype(BF16), v_ref[0, rows(i), v_cols])
            if i > 0:
                y = y + _dot(qi * xi_f, sf_scr[hh, i])
            if i < n_chunks - 1:
                y = y + _dot(qi * xi_b, sb_scr[hh, i])
            mu = jnp.mean(y, axis=-1, keepdims=True)
            yc = y - mu
            var = jnp.mean(yc * yc, axis=-1, keepdims=True)
            yn = (yc * lax.rsqrt(var + GN_EPS)).astype(BF16)
            y_ref[0, rows(i), v_cols] = gs_ref[0, rows(i), v_cols] * yn


def _retention(q, kt, v, gs, dec_f, dec_b):
    batch, seq, _ = q.shape
    assert RET_CHUNK == RET_QK_DIM and seq % RET_CHUNK == 0
    hps = RET_HEADS_PER_STEP
    grid_spec = pltpu.PrefetchScalarGridSpec(
        num_scalar_prefetch=2,
        grid=(batch, RET_HEADS // hps),
        in_specs=[
            pl.BlockSpec((1, seq, hps * RET_QK_DIM), lambda b, h, *_: (b, 0, h)),
            pl.BlockSpec((1, hps * RET_QK_DIM, seq), lambda b, h, *_: (b, h, 0)),
            pl.BlockSpec((1, seq, hps * RET_V_DIM), lambda b, h, *_: (b, 0, h)),
            pl.BlockSpec((1, seq, hps * RET_V_DIM), lambda b, h, *_: (b, 0, h)),
        ],
        out_specs=pl.BlockSpec((1, seq, hps * RET_V_DIM), lambda b, h, *_: (b, 0, h)),
        scratch_shapes=[pltpu.VMEM((hps, seq // RET_CHUNK, RET_QK_DIM, RET_V_DIM), BF16)] * 2,
    )
    return pl.pallas_call(
        _retention_kernel,
        grid_spec=grid_spec,
        out_shape=jax.ShapeDtypeStruct((batch, seq, RET_V_W), BF16),
        compiler_params=pltpu.CompilerParams(
            dimension_semantics=("arbitrary", "arbitrary"), vmem_limit_bytes=V7X_VMEM_LIMIT_BYTES),
        name="retention",
    )(dec_f, dec_b, q, kt, v, gs)


def _mix_out_mlp(y_ref, h_ref, wo_ref, gmlp_ref, w1_ref, w2_ref):
    h1 = h_ref[...] + _dot(y_ref[...], wo_ref[...])
    u = (h1 * gmlp_ref[...]).astype(BF16)
    r = _rms_scale(h1)
    mlp = None
    d_ff = w1_ref.shape[1]
    for c in range(d_ff // FF_CHUNK):
        lo = c * FF_CHUNK
        a = jnp.maximum(_dot(u, w1_ref[:, lo:lo + FF_CHUNK]), 0.0)
        part = _dot((a * a).astype(BF16), w2_ref[lo:lo + FF_CHUNK, :])
        mlp = part if mlp is None else mlp + part
    return h1 + (r * r) * mlp


def _tail_attn_proj_kernel(y_ref, h_ref, wo_ref, gmlp_ref, w1_ref, w2_ref,
                           gmix_ref, wa_ref, qn_ref, kn_ref,
                           cos2_ref, sin2_ref, cost_ref, sint_ref,
                           hout_ref, qt_ref, k_ref, vt_ref, wqt_scr, wvt_scr):
    k_col0, v_col0 = ATTN_Q_W, ATTN_Q_W + ATTN_KV_W

    @pl.when(pl.program_id(0) == 0)
    def _():
        blk = 2 * ATTN_HEAD_DIM
        for lo in range(0, ATTN_Q_W, blk):
            wqt_scr[lo:lo + blk, :] = wa_ref[:, lo:lo + blk].T
        wvt_scr[...] = wa_ref[:, v_col0:v_col0 + ATTN_KV_W].T

    h2 = _mix_out_mlp(y_ref, h_ref, wo_ref, gmlp_ref, w1_ref, w2_ref)
    hout_ref[...] = h2
    u = (h2 * gmix_ref[...]).astype(BF16)
    r = _rms_scale(h2)
    d = ATTN_HEAD_DIM
    half = d // 2
    r_t = jnp.broadcast_to(r, (h2.shape[0], d)).T[:1]

    cos2 = cos2_ref[...]
    sin2 = sin2_ref[...]
    k2 = _dot(u, wa_ref[:, k_col0:k_col0 + ATTN_KV_W]) * r
    for hh in range(ATTN_KV_HEADS):
        kn = _rms_norm_rows(k2[:, hh * d:(hh + 1) * d], kn_ref[...])
        k_ref[:, hh * d:(hh + 1) * d] = (kn * cos2 + pltpu.roll(kn, half, 1) * sin2).astype(BF16)
    vt = _dot_nt(wvt_scr[...], u)
    for hh in range(ATTN_KV_HEADS):
        vt_ref[0, hh, :d, :] = (vt[hh * d:(hh + 1) * d] * r_t).astype(BF16)
        vt_ref[0, hh, d:, :] = jnp.ones((ATTN_VT_ROWS - d, vt.shape[1]), BF16)

    cos_t = cost_ref[...]
    sin_t = sint_ref[...]
    gain_t = qn_ref[...]
    for pair in range(ATTN_Q_HEADS // 2):
        q2 = _dot_nt(wqt_scr[pair * 2 * d:(pair + 1) * 2 * d, :], u)
        for hh in range(2):
            x = q2[hh * d:(hh + 1) * d]
            ms = jnp.mean(x * x, axis=0, keepdims=True) * (r_t * r_t)
            xn = x * (lax.rsqrt(ms + RMS_EPS) * r_t * ATTN_Q_SCALE) * gain_t
            x1 = xn[:half]
            x2 = xn[half:]
            lo = (pair * 2 + hh) * d
            qt_ref[0, lo:lo + half, :] = (x1 * cos_t - x2 * sin_t).astype(BF16)
            qt_ref[0, lo + half:lo + d, :] = (x1 * sin_t + x2 * cos_t).astype(BF16)


def _tail_final_kernel(y_ref, h_ref, wo_ref, gmlp_ref, w1_ref, w2_ref, gfin_ref, out_ref):
    h2 = _mix_out_mlp(y_ref, h_ref, wo_ref, gmlp_ref, w1_ref, w2_ref)
    out_ref[...] = _rms_norm_rows(h2, gfin_ref[...])


def _tail_common_specs(y, h, wo, w1, w2, tm):
    d = h.shape[1]
    return [
        pl.BlockSpec((tm, y.shape[1]), lambda i: (i, 0)),
        pl.BlockSpec((tm, d), lambda i: (i, 0)),
        _resident(wo.shape),
        _resident((1, d)),
        _resident(w1.shape),
        _resident(w2.shape),
    ]


def _tail_attn_proj(y, h, wo, gmlp, w1, w2, gmix, wa, qn_t, kn, cos2, sin2, cos_t, sin_t, batch, seq):
    tokens, d = h.shape
    tm = TOKEN_TILE
    tiles_per_seq = seq // tm
    half = ATTN_HEAD_DIM // 2
    pos_tile = pl.BlockSpec((tm, ATTN_HEAD_DIM), lambda i: (i % tiles_per_seq, 0))
    pos_tile_t = pl.BlockSpec((half, tm), lambda i: (0, i % tiles_per_seq))
    return pl.pallas_call(
        _tail_attn_proj_kernel,
        grid=(tokens // tm,),
        in_specs=_tail_common_specs(y, h, wo, w1, w2, tm) + [
            _resident((1, d)),
            _resident(wa.shape),
            _resident(qn_t.shape),
            _resident((1, ATTN_HEAD_DIM)),
            pos_tile,
            pos_tile,
            pos_tile_t,
            pos_tile_t,
        ],
        out_specs=[
            pl.BlockSpec((tm, d), lambda i: (i, 0)),
            pl.BlockSpec((1, ATTN_Q_W, tm), lambda i: (i // tiles_per_seq, 0, i % tiles_per_seq)),
            pl.BlockSpec((tm, ATTN_KV_W), lambda i: (i, 0)),
            pl.BlockSpec((1, ATTN_KV_HEADS, ATTN_VT_ROWS, tm),
                         lambda i: (i // tiles_per_seq, 0, 0, i % tiles_per_seq)),
        ],
        out_shape=[
            jax.ShapeDtypeStruct((tokens, d), F32),
            jax.ShapeDtypeStruct((batch, ATTN_Q_W, seq), BF16),
            jax.ShapeDtypeStruct((tokens, ATTN_KV_W), BF16),
            jax.ShapeDtypeStruct((batch, ATTN_KV_HEADS, ATTN_VT_ROWS, seq), BF16),
        ],
        scratch_shapes=[pltpu.VMEM((ATTN_Q_W, d), BF16), pltpu.VMEM((ATTN_KV_W, d), BF16)],
        compiler_params=pltpu.CompilerParams(
            dimension_semantics=("arbitrary",), vmem_limit_bytes=V7X_VMEM_LIMIT_BYTES),
        name="tail_attn_proj",
    )(y, h, wo, gmlp, w1, w2, gmix, wa, qn_t, kn, cos2, sin2, cos_t, sin_t)


def _tail_final(y, h, wo, gmlp, w1, w2, gfin):
    tokens, d = h.shape
    tm = WIDE_TOKEN_TILE
    return pl.pallas_call(
        _tail_final_kernel,
        grid=(tokens // tm,),
        in_specs=_tail_common_specs(y, h, wo, w1, w2, tm) + [_resident((1, d))],
        out_specs=pl.BlockSpec((tm, d), lambda i: (i, 0)),
        out_shape=jax.ShapeDtypeStruct((tokens, d), F32),
        compiler_params=pltpu.CompilerParams(
            dimension_semantics=("arbitrary",), vmem_limit_bytes=V7X_VMEM_LIMIT_BYTES),
        name="tail_final",
    )(y, h, wo, gmlp, w1, w2, gfin)


def _attention_kernel(noshift_ref, qt_ref, k_ref, vt_ref, o_ref, s_scr, p_scr):
    d = ATTN_HEAD_DIM
    seq = k_ref.shape[1]
    tq = ATTN_Q_TILE
    kc = ATTN_KV_CHUNK
    units = [(j, g) for j in range(ATTN_Q_TILES_PER_STEP) for g in range(ATTN_GROUP)]

    def scores(u):
        j, g = units[u]
        qt = qt_ref[0, g * d:(g + 1) * d, j * tq:(j + 1) * tq]
        for c in range(seq // kc):
            s_scr[u % 2, c * kc:(c + 1) * kc, :] = _dot(k_ref[0, c * kc:(c + 1) * kc, :], qt)

    def softmax(u):
        m = None
        for c in range(seq // kc):
            mc = jnp.max(s_scr[u % 2, c * kc:(c + 1) * kc, :], axis=0, keepdims=True)
            m = mc if m is None else jnp.maximum(m, mc)
        for c in range(seq // kc):
            p = jnp.exp2(s_scr[u % 2, c * kc:(c + 1) * kc, :] - m)
            p_scr[u % 2, c * kc:(c + 1) * kc, :] = p.astype(BF16)

    def finish(u):
        j, g = units[u]
        acc = _dot(vt_ref[0, 0], p_scr[u % 2])
        o_ref[0, j * tq:(j + 1) * tq, g * d:(g + 1) * d] = (acc[:d] / acc[d:d + 1]).T.astype(BF16)

    def scores_exp(u):
        j, g = units[u]
        qt = qt_ref[0, g * d:(g + 1) * d, j * tq:(j + 1) * tq]
        l = None
        for c in range(seq // kc):
            p = jnp.exp2(_dot(k_ref[0, c * kc:(c + 1) * kc, :], qt))
            p_scr[u % 2, c * kc:(c + 1) * kc, :] = p.astype(BF16)
            lc = jnp.sum(p, axis=0, keepdims=True)
            l = lc if l is None else l + lc
        return l

    def finish_noshift(u, l):
        j, g = units[u]
        acc = _dot(vt_ref[0, 0, :d, :], p_scr[u % 2])
        o_ref[0, j * tq:(j + 1) * tq, g * d:(g + 1) * d] = (acc / l).T.astype(BF16)

    @pl.when(noshift_ref[0] == 1)
    def _():
        l_next = scores_exp(0)
        for u in range(len(units)):
            l = l_next
            if u + 1 < len(units):
                l_next = scores_exp(u + 1)
            finish_noshift(u, l)

    @pl.when(noshift_ref[0] != 1)
    def _():
        scores(0)
        for u in range(len(units)):
            if u + 1 < len(units):
                scores(u + 1)
            if u > 0:
                finish(u - 1)
            softmax(u)
        finish(len(units) - 1)


def _attention(noshift, qt, k, vt):
    batch, _, seq = qt.shape
    group_w = ATTN_GROUP * ATTN_HEAD_DIM
    tq_step = ATTN_Q_TILE * ATTN_Q_TILES_PER_STEP
    grid_spec = pltpu.PrefetchScalarGridSpec(
        num_scalar_prefetch=1,
        grid=(batch, ATTN_KV_HEADS, seq // tq_step),
        in_specs=[
            pl.BlockSpec((1, group_w, tq_step), lambda b, kk, t, *_: (b, kk, t)),
            pl.BlockSpec((1, seq, ATTN_HEAD_DIM), lambda b, kk, t, *_: (b, 0, kk)),
            pl.BlockSpec((1, 1, ATTN_VT_ROWS, seq), lambda b, kk, t, *_: (b, kk, 0, 0)),
        ],
        out_specs=pl.BlockSpec((1, tq_step, group_w), lambda b, kk, t, *_: (b, t, kk)),
        scratch_shapes=[pltpu.VMEM((2, seq, ATTN_Q_TILE), F32), pltpu.VMEM((2, seq, ATTN_Q_TILE), BF16)],
    )
    return pl.pallas_call(
        _attention_kernel,
        grid_spec=grid_spec,
        out_shape=jax.ShapeDtypeStruct((batch, seq, ATTN_Q_W), BF16),
        compiler_params=pltpu.CompilerParams(
            dimension_semantics=("arbitrary", "arbitrary", "arbitrary"),
            vmem_limit_bytes=V7X_VMEM_LIMIT_BYTES),
        name="attention",
    )(noshift, qt, k, vt)


def kernel(x, norm_mix, norm_mlp, mlp_w1, mlp_w2, ret_w_in, ret_w_out, ret_decay_fwd, ret_decay_bwd,
           attn_w_in, attn_w_out, attn_q_norm, attn_k_norm, final_norm):
    batch, seq, d = x.shape
    assert norm_mix.shape[0] == 2 and ret_w_in.shape[0] == 1 and attn_w_in.shape[0] == 1
    assert seq % WIDE_TOKEN_TILE == 0 and seq % TOKEN_TILE == 0 and seq % GRID_W == 0
    tokens = batch * seq
    h = x.reshape(tokens, d)

    cos_r, sin_r = _axial_rope_tables(seq, RET_QK_DIM)
    cos_a, sin_a = _axial_rope_tables(seq, ATTN_HEAD_DIM)
    cos2 = np.concatenate([cos_a, cos_a], axis=-1)
    sin2 = np.concatenate([-sin_a, sin_a], axis=-1)

    q, kt, v, gs = _ret_proj(h, norm_mix[0][None, :], ret_w_in[0].astype(BF16),
                             (cos_r, sin_r, np.ascontiguousarray(cos_r.T), np.ascontiguousarray(sin_r.T)),
                             batch, seq)
    y = _retention(q.reshape(batch, seq, RET_QK_W), kt, v.reshape(batch, seq, RET_V_W),
                   gs.reshape(batch, seq, RET_V_W), ret_decay_fwd[0], ret_decay_bwd[0])
    h, qta, ka, vta = _tail_attn_proj(
        y.reshape(tokens, RET_V_W), h, ret_w_out[0].astype(BF16), norm_mlp[0][None, :],
        mlp_w1[0].astype(BF16), mlp_w2[0].astype(BF16), norm_mix[1][None, :],
        attn_w_in[0].astype(BF16),
        jnp.broadcast_to(attn_q_norm[0][:, None], (ATTN_HEAD_DIM, TOKEN_TILE)), attn_k_norm[0][None, :],
        cos2, sin2, np.ascontiguousarray(cos_a.T), np.ascontiguousarray(sin_a.T), batch, seq)

    score_bound = (ATTN_HEAD_DIM * ATTN_Q_SCALE * 1.01) * (
        jnp.max(jnp.abs(attn_q_norm[0])) * jnp.max(jnp.abs(attn_k_norm[0])))
    noshift = (score_bound <= ATTN_NOSHIFT_MAX_LOG2).astype(jnp.int32).reshape(1)
    o = _attention(noshift, qta, ka.reshape(batch, seq, ATTN_KV_W), vta)
    out = _tail_final(o.reshape(tokens, d), h, attn_w_out[0].astype(BF16), norm_mlp[1][None, :],
                      mlp_w1[1].astype(BF16), mlp_w2[1].astype(BF16), final_norm[None, :])
    return out.reshape(batch, seq, d)
```

```python
import math

import jax
import jax.numpy as jnp
import numpy as np
from jax import lax
from jax.experimental import pallas as pl
from jax.experimental.pallas import tpu as pltpu

BF16 = jnp.bfloat16
F32 = jnp.float32

GRID_W = 64
RMS_EPS = 1e-6
GN_EPS = 1e-6
ROPE_THETA = 10000.0

RET_HEADS = 4
RET_QK_DIM = 256
RET_V_DIM = 512
RET_QK_W = RET_HEADS * RET_QK_DIM
RET_V_W = RET_HEADS * RET_V_DIM
RET_CHUNK = 256
RET_HEADS_PER_STEP = 2

ATTN_HEAD_DIM = 128
ATTN_Q_HEADS = 8
ATTN_KV_HEADS = 2
ATTN_GROUP = ATTN_Q_HEADS // ATTN_KV_HEADS
ATTN_Q_W = ATTN_Q_HEADS * ATTN_HEAD_DIM
ATTN_KV_W = ATTN_KV_HEADS * ATTN_HEAD_DIM

TOKEN_TILE = 512
WIDE_TOKEN_TILE = 1024
ATTN_Q_TILE = 256
ATTN_Q_TILES_PER_STEP = 8
ATTN_KV_CHUNK = 256
ATTN_VT_ROWS = ATTN_HEAD_DIM + 16
ATTN_Q_SCALE = ATTN_HEAD_DIM ** -0.5 * math.log2(math.e)
ATTN_NOSHIFT_MAX_LOG2 = 64.0
FF_CHUNK = 1024
V7X_VMEM_LIMIT_BYTES = 56 * 1024 * 1024


def _dot(a, b):
    return jnp.dot(a, b, preferred_element_type=F32)


def _dot_nt(a, b):
    return lax.dot_general(a, b, (((1,), (1,)), ((), ())), preferred_element_type=F32)


def _rms_scale(x):
    return lax.rsqrt(jnp.mean(x * x, axis=-1, keepdims=True) + RMS_EPS)


def _rms_norm_rows(x, gain):
    return x * _rms_scale(x) * gain


def _resident(shape):
    nd = len(shape)
    return pl.BlockSpec(shape, lambda *_: (0,) * nd, pipeline_mode=pl.Buffered(1))


def _axial_rope_tables(seq, rot_dim):
    rows = seq // GRID_W
    row = np.repeat(np.arange(rows, dtype=np.float64), GRID_W)
    col = np.tile(np.arange(GRID_W, dtype=np.float64), rows)
    per_axis = rot_dim // 2
    n_freq = per_axis // 2
    inv_freq = ROPE_THETA ** (-np.arange(n_freq, dtype=np.float64) * 2.0 / per_axis)
    ang = np.concatenate([row[:, None] * inv_freq[None, :], col[:, None] * inv_freq[None, :]], axis=-1)
    return np.cos(ang).astype(np.float32), np.sin(ang).astype(np.float32)


def _ret_proj_kernel(h_ref, gain_ref, w_ref, cos_ref, sin_ref, cost_ref, sint_ref,
                     q_ref, kt_ref, v_ref, gs_ref, wkt_scr):
    k_col0, v_col0, g_col0 = RET_QK_W, 2 * RET_QK_W, 2 * RET_QK_W + RET_V_W

    @pl.when(pl.program_id(0) == 0)
    def _():
        for hh in range(RET_HEADS):
            lo = hh * RET_QK_DIM
            wkt_scr[lo:lo + RET_QK_DIM, :] = w_ref[:, k_col0 + lo:k_col0 + lo + RET_QK_DIM].T

    x = h_ref[...]
    u = (x * gain_ref[...]).astype(BF16)
    r = _rms_scale(x)
    half = RET_QK_DIM // 2
    r_t = jnp.broadcast_to(r, (x.shape[0], half)).T
    for c in range(RET_HEADS):
        lo = c * RET_V_DIM
        g = _dot(u, w_ref[:, g_col0 + lo:g_col0 + lo + RET_V_DIM]) * r
        gs_ref[:, lo:lo + RET_V_DIM] = (g * jax.nn.sigmoid(g)).astype(BF16)
    k_scale = RET_QK_DIM ** -0.5
    cos_t = cost_ref[...] * (r_t * k_scale)
    sin_t = sint_ref[...] * (r_t * k_scale)
    for hh in range(RET_HEADS):
        lo = hh * RET_QK_DIM
        kh = _dot_nt(wkt_scr[lo:lo + RET_QK_DIM, :], u)
        x1 = kh[:half]
        x2 = kh[half:]
        kt_ref[0, lo:lo + half, :] = (x1 * cos_t - x2 * sin_t).astype(BF16)
        kt_ref[0, lo + half:lo + RET_QK_DIM, :] = (x1 * sin_t + x2 * cos_t).astype(BF16)
    cos = cos_ref[...] * r
    sin = sin_ref[...] * r
    for hh in range(RET_HEADS):
        lo = hh * RET_QK_DIM
        qh = _dot(u, w_ref[:, lo:lo + RET_QK_DIM])
        x1 = qh[:, :half]
        x2 = qh[:, half:]
        q_ref[:, lo:lo + half] = (x1 * cos - x2 * sin).astype(BF16)
        q_ref[:, lo + half:lo + RET_QK_DIM] = (x1 * sin + x2 * cos).astype(BF16)
    for c in range(RET_HEADS):
        lo = c * RET_V_DIM
        v_ref[:, lo:lo + RET_V_DIM] = (_dot(u, w_ref[:, v_col0 + lo:v_col0 + lo + RET_V_DIM]) * r).astype(BF16)


def _ret_proj(h, gain, w, cos_tabs, batch, seq):
    cos, sin, cos_t, sin_t = cos_tabs
    tokens, d = h.shape
    tm = WIDE_TOKEN_TILE
    tiles_per_seq = seq // tm
    half = RET_QK_DIM // 2
    row_tile = lambda width: pl.BlockSpec((tm, width), lambda i: (i, 0))
    return pl.pallas_call(
        _ret_proj_kernel,
        grid=(tokens // tm,),
        in_specs=[
            row_tile(d),
            _resident((1, d)),
            _resident(w.shape),
            pl.BlockSpec((tm, half), lambda i: (i % tiles_per_seq, 0)),
            pl.BlockSpec((tm, half), lambda i: (i % tiles_per_seq, 0)),
            pl.BlockSpec((half, tm), lambda i: (0, i % tiles_per_seq)),
            pl.BlockSpec((half, tm), lambda i: (0, i % tiles_per_seq)),
        ],
        out_specs=[
            row_tile(RET_QK_W),
            pl.BlockSpec((1, RET_QK_W, tm), lambda i: (i // tiles_per_seq, 0, i % tiles_per_seq)),
            row_tile(RET_V_W),
            row_tile(RET_V_W),
        ],
        out_shape=[
            jax.ShapeDtypeStruct((tokens, RET_QK_W), BF16),
            jax.ShapeDtypeStruct((batch, RET_QK_W, seq), BF16),
            jax.ShapeDtypeStruct((tokens, RET_V_W), BF16),
            jax.ShapeDtypeStruct((tokens, RET_V_W), BF16),
        ],
        scratch_shapes=[pltpu.VMEM((RET_QK_W, d), BF16)],
        compiler_params=pltpu.CompilerParams(
            dimension_semantics=("arbitrary",), vmem_limit_bytes=V7X_VMEM_LIMIT_BYTES),
        name="ret_proj",
    )(h, gain, w, cos, sin, cos_t, sin_t)


def _retention_kernel(dec_f_ref, dec_b_ref, q_ref, kt_ref, v_ref, gs_ref, y_ref, sf_scr, sb_scr):
    c = RET_CHUNK
    dk = RET_QK_DIM
    dv = RET_V_DIM
    n_chunks = q_ref.shape[1] // c
    row = lax.broadcasted_iota(jnp.int32, (c, c), 0).astype(F32)
    col = lax.broadcasted_iota(jnp.int32, (c, c), 1).astype(F32)
    diff = row - col

    def rows(i):
        return slice(i * c, (i + 1) * c)

    for hh in range(RET_HEADS_PER_STEP):
        head = pl.program_id(1) * RET_HEADS_PER_STEP + hh
        qk_cols = slice(hh * dk, (hh + 1) * dk)
        v_cols = slice(hh * dv, (hh + 1) * dv)
        lgf = -jnp.exp(jnp.full((c, c), dec_f_ref[head], F32))
        lgb = -jnp.exp(jnp.full((c, c), dec_b_ref[head], F32))
        decay = jnp.where(diff >= 0, jnp.exp(lgf * jnp.maximum(diff, 0.0)),
                          jnp.exp(lgb * jnp.maximum(-diff, 0.0)))
        xi_f = jnp.exp(lgf * (row + 1.0)).astype(BF16)
        xi_b = jnp.exp(lgb * (float(c) - row)).astype(BF16)
        zeta_f = jnp.exp(lgf * (float(c) - 1.0 - col)).astype(BF16)
        zeta_b = jnp.exp(lgb * col).astype(BF16)
        gc_f = jnp.exp(-jnp.exp(jnp.full((dk, dv), dec_f_ref[head], F32)) * float(c))
        gc_b = jnp.exp(-jnp.exp(jnp.full((dk, dv), dec_b_ref[head], F32)) * float(c))

        state = None
        for i in range(n_chunks - 1):
            upd = _dot(kt_ref[0, qk_cols, rows(i)] * zeta_f, v_ref[0, rows(i), v_cols])
            state = upd if state is None else state * gc_f + upd
            sf_scr[hh, i + 1] = state.astype(BF16)
        state = None
        for i in range(n_chunks - 1, 0, -1):
            upd = _dot(kt_ref[0, qk_cols, rows(i)] * zeta_b, v_ref[0, rows(i), v_cols])
            state = upd if state is None else state * gc_b + upd
            sb_scr[hh, i - 1] = state.astype(BF16)

        for i in range(n_chunks):
            qi = q_ref[0, rows(i), qk_cols]
            scores = _dot(qi, kt_ref[0, qk_cols, rows(i)]) * decay
            y = _dot(scores.astype(BF16), v_ref[0, rows(i), v_cols])
            if i > 0:
                y = y + _dot(qi * xi_f, sf_scr[hh, i])
            if i < n_chunks - 1:
                y = y + _dot(qi * xi_b, sb_scr[hh, i])
            mu = jnp.mean(y, axis=-1, keepdims=True)
            yc = y - mu
            var = jnp.mean(yc * yc, axis=-1, keepdims=True)
            yn = (yc * lax.rsqrt(var + GN_EPS)).astype(BF16)
            y_ref[0, rows(i), v_cols] = gs_ref[0, rows(i), v_cols] * yn


def _retention(q, kt, v, gs, dec_f, dec_b):
    batch, seq, _ = q.shape
    assert RET_CHUNK == RET_QK_DIM and seq % RET_CHUNK == 0
    hps = RET_HEADS_PER_STEP
    grid_spec = pltpu.PrefetchScalarGridSpec(
        num_scalar_prefetch=2,
        grid=(batch, RET_HEADS // hps),
        in_specs=[
            pl.BlockSpec((1, seq, hps * RET_QK_DIM), lambda b, h, *_: (b, 0, h)),
            pl.BlockSpec((1, hps * RET_QK_DIM, seq), lambda b, h, *_: (b, h, 0)),
            pl.BlockSpec((1, seq, hps * RET_V_DIM), lambda b, h, *_: (b, 0, h)),
            pl.BlockSpec((1, seq, hps * RET_V_DIM), lambda b, h, *_: (b, 0, h)),
        ],
        out_specs=pl.BlockSpec((1, seq, hps * RET_V_DIM), lambda b, h, *_: (b, 0, h)),
        scratch_shapes=[pltpu.VMEM((hps, seq // RET_CHUNK, RET_QK_DIM, RET_V_DIM), BF16)] * 2,
    )
    return pl.pallas_call(
        _retention_kernel,
        grid_spec=grid_spec,
        out_shape=jax.ShapeDtypeStruct((batch, seq, RET_V_W), BF16),
        compiler_params=pltpu.CompilerParams(
            dimension_semantics=("arbitrary", "arbitrary"), vmem_limit_bytes=V7X_VMEM_LIMIT_BYTES),
        name="retention",
    )(dec_f, dec_b, q, kt, v, gs)


def _mix_out_mlp(y_ref, h_ref, wo_ref, gmlp_ref, w1_ref, w2_ref):
    h1 = h_ref[...] + _dot(y_ref[...], wo_ref[...])
    u = (h1 * gmlp_ref[...]).astype(BF16)
    r = _rms_scale(h1)
    mlp = None
    d_ff = w1_ref.shape[2]
    for c in range(d_ff // FF_CHUNK):
        lo = c * FF_CHUNK
        a = jnp.maximum(_dot(u, w1_ref[0, :, lo:lo + FF_CHUNK]), 0.0)
        part = _dot((a * a).astype(BF16), w2_ref[0, lo:lo + FF_CHUNK, :])
        mlp = part if mlp is None else mlp + part
    return h1 + (r * r) * mlp


def _tail_attn_proj_kernel(y_ref, h_ref, wo_ref, gmlp_ref, w1_ref, w2_ref,
                           gmix_ref, wa_ref, qn_ref, kn_ref,
                           cos2_ref, sin2_ref, cost_ref, sint_ref,
                           hout_ref, qt_ref, k_ref, vt_ref, wqvt_scr):
    k_col0, v_col0 = ATTN_Q_W, ATTN_Q_W + ATTN_KV_W

    @pl.when(pl.program_id(0) == 0)
    def _():
        blk = 2 * ATTN_HEAD_DIM
        wqvt_scr[0:ATTN_KV_W, :] = wa_ref[:, v_col0:v_col0 + ATTN_KV_W].T
        for lo in range(0, ATTN_Q_W, blk):
            wqvt_scr[ATTN_KV_W + lo:ATTN_KV_W + lo + blk, :] = wa_ref[:, lo:lo + blk].T

    h2 = _mix_out_mlp(y_ref, h_ref, wo_ref, gmlp_ref, w1_ref, w2_ref)
    hout_ref[...] = h2
    u = (h2 * gmix_ref[...]).astype(BF16)
    r = _rms_scale(h2)
    d = ATTN_HEAD_DIM
    half = d // 2
    r_t = jnp.broadcast_to(r, (h2.shape[0], d)).T[:1]

    cos2 = cos2_ref[...]
    sin2 = sin2_ref[...]
    k2 = _dot(u, wa_ref[:, k_col0:k_col0 + ATTN_KV_W]) * r
    for hh in range(ATTN_KV_HEADS):
        kn = _rms_norm_rows(k2[:, hh * d:(hh + 1) * d], kn_ref[...])
        k_ref[:, hh * d:(hh + 1) * d] = (kn * cos2 + pltpu.roll(kn, half, 1) * sin2).astype(BF16)
    qv = _dot_nt(wqvt_scr[...], u)
    vt = qv[:ATTN_KV_W]
    for hh in range(ATTN_KV_HEADS):
        vt_ref[0, hh, :d, :] = (vt[hh * d:(hh + 1) * d] * r_t).astype(BF16)
        vt_ref[0, hh, d:, :] = jnp.ones((ATTN_VT_ROWS - d, vt.shape[1]), BF16)

    cos_t = cost_ref[...]
    sin_t = sint_ref[...]
    gain_t = qn_ref[...]
    for pair in range(ATTN_Q_HEADS // 2):
        q2 = qv[ATTN_KV_W + pair * 2 * d:ATTN_KV_W + (pair + 1) * 2 * d]
        for hh in range(2):
            x = q2[hh * d:(hh + 1) * d]
            ms = jnp.mean(x * x, axis=0, keepdims=True) * (r_t * r_t)
            xn = x * (lax.rsqrt(ms + RMS_EPS) * r_t * ATTN_Q_SCALE) * gain_t
            x1 = xn[:half]
            x2 = xn[half:]
            lo = (pair * 2 + hh) * d
            qt_ref[0, lo:lo + half, :] = (x1 * cos_t - x2 * sin_t).astype(BF16)
            qt_ref[0, lo + half:lo + d, :] = (x1 * sin_t + x2 * cos_t).astype(BF16)


def _tail_final_kernel(y_ref, h_ref, wo_ref, gmlp_ref, w1_ref, w2_ref, gfin_ref, out_ref):
    h2 = _mix_out_mlp(y_ref, h_ref, wo_ref, gmlp_ref, w1_ref, w2_ref)
    out_ref[...] = _rms_norm_rows(h2, gfin_ref[...])


def _tail_common_specs(y, h, wo, w1, w2, layer, tm):
    d = h.shape[1]
    layer_slab = lambda w: pl.BlockSpec((1,) + w.shape[1:], lambda *_: (layer, 0, 0),
                                        pipeline_mode=pl.Buffered(1))
    return [
        pl.BlockSpec((tm, y.shape[1]), lambda i: (i, 0)),
        pl.BlockSpec((tm, d), lambda i: (i, 0)),
        _resident(wo.shape),
        _resident((1, d)),
        layer_slab(w1),
        layer_slab(w2),
    ]


def _tail_attn_proj(y, h, wo, gmlp, w1, w2, layer, gmix, wa, qn_t, kn, cos2, sin2, cos_t, sin_t,
                    batch, seq):
    tokens, d = h.shape
    tm = TOKEN_TILE
    tiles_per_seq = seq // tm
    half = ATTN_HEAD_DIM // 2
    pos_tile = pl.BlockSpec((tm, ATTN_HEAD_DIM), lambda i: (i % tiles_per_seq, 0))
    pos_tile_t = pl.BlockSpec((half, tm), lambda i: (0, i % tiles_per_seq))
    return pl.pallas_call(
        _tail_attn_proj_kernel,
        grid=(tokens // tm,),
        in_specs=_tail_common_specs(y, h, wo, w1, w2, layer, tm) + [
            _resident((1, d)),
            _resident(wa.shape),
            _resident(qn_t.shape),
            _resident((1, ATTN_HEAD_DIM)),
            pos_tile,
            pos_tile,
            pos_tile_t,
            pos_tile_t,
        ],
        out_specs=[
            pl.BlockSpec((tm, d), lambda i: (i, 0)),
            pl.BlockSpec((1, ATTN_Q_W, tm), lambda i: (i // tiles_per_seq, 0, i % tiles_per_seq)),
            pl.BlockSpec((tm, ATTN_KV_W), lambda i: (i, 0)),
            pl.BlockSpec((1, ATTN_KV_HEADS, ATTN_VT_ROWS, tm),
                         lambda i: (i // tiles_per_seq, 0, 0, i % tiles_per_seq)),
        ],
        out_shape=[
            jax.ShapeDtypeStruct((tokens, d), F32),
            jax.ShapeDtypeStruct((batch, ATTN_Q_W, seq), BF16),
            jax.ShapeDtypeStruct((tokens, ATTN_KV_W), BF16),
            jax.ShapeDtypeStruct((batch, ATTN_KV_HEADS, ATTN_VT_ROWS, seq), BF16),
        ],
        scratch_shapes=[pltpu.VMEM((ATTN_KV_W + ATTN_Q_W, d), BF16)],
        compiler_params=pltpu.CompilerParams(
            dimension_semantics=("arbitrary",), vmem_limit_bytes=V7X_VMEM_LIMIT_BYTES),
        name="tail_attn_proj",
    )(y, h, wo, gmlp, w1, w2, gmix, wa, qn_t, kn, cos2, sin2, cos_t, sin_t)


def _tail_final(y, h, wo, gmlp, w1, w2, layer, gfin):
    tokens, d = h.shape
    tm = WIDE_TOKEN_TILE
    return pl.pallas_call(
        _tail_final_kernel,
        grid=(tokens // tm,),
        in_specs=_tail_common_specs(y, h, wo, w1, w2, layer, tm) + [_resident((1, d))],
        out_specs=pl.BlockSpec((tm, d), lambda i: (i, 0)),
        out_shape=jax.ShapeDtypeStruct((tokens, d), F32),
        compiler_params=pltpu.CompilerParams(
            dimension_semantics=("arbitrary",), vmem_limit_bytes=V7X_VMEM_LIMIT_BYTES),
        name="tail_final",
    )(y, h, wo, gmlp, w1, w2, gfin)


def _attention_kernel(noshift_ref, qt_ref, k_ref, vt_ref, o_ref, s_scr, p_scr):
    d = ATTN_HEAD_DIM
    seq = k_ref.shape[1]
    tq = ATTN_Q_TILE
    kc = ATTN_KV_CHUNK
    units = [(j, g) for j in range(ATTN_Q_TILES_PER_STEP) for g in range(ATTN_GROUP)]

    def scores(u):
        j, g = units[u]
        qt = qt_ref[0, g * d:(g + 1) * d, j * tq:(j + 1) * tq]
        for c in range(seq // kc):
            s_scr[u % 2, c * kc:(c + 1) * kc, :] = _dot(k_ref[0, c * kc:(c + 1) * kc, :], qt)

    def softmax(u):
        m = None
        for c in range(seq // kc):
            mc = jnp.max(s_scr[u % 2, c * kc:(c + 1) * kc, :], axis=0, keepdims=True)
            m = mc if m is None else jnp.maximum(m, mc)
        for c in range(seq // kc):
            p = jnp.exp2(s_scr[u % 2, c * kc:(c + 1) * kc, :] - m)
            p_scr[u % 2, c * kc:(c + 1) * kc, :] = p.astype(BF16)

    def finish(u):
        j, g = units[u]
        acc = _dot(vt_ref[0, 0], p_scr[u % 2])
        o_ref[0, j * tq:(j + 1) * tq, g * d:(g + 1) * d] = (acc[:d] / acc[d:d + 1]).T.astype(BF16)

    def scores_exp(u):
        j, g = units[u]
        qt = qt_ref[0, g * d:(g + 1) * d, j * tq:(j + 1) * tq]
        l = None
        for c in range(seq // kc):
            p = jnp.exp2(_dot(k_ref[0, c * kc:(c + 1) * kc, :], qt))
            p_scr[u % 2, c * kc:(c + 1) * kc, :] = p.astype(BF16)
            lc = jnp.sum(p, axis=0, keepdims=True)
            l = lc if l is None else l + lc
        return l

    def finish_noshift(u, l):
        j, g = units[u]
        acc = _dot(vt_ref[0, 0, :d, :], p_scr[u % 2])
        o_ref[0, j * tq:(j + 1) * tq, g * d:(g + 1) * d] = (acc / l).T.astype(BF16)

    @pl.when(noshift_ref[0] == 1)
    def _():
        l_next = scores_exp(0)
        for u in range(len(units)):
            l = l_next
            if u + 1 < len(units):
                l_next = scores_exp(u + 1)
            finish_noshift(u, l)

    @pl.when(noshift_ref[0] != 1)
    def _():
        scores(0)
        for u in range(len(units)):
            if u + 1 < len(units):
                scores(u + 1)
            if u > 0:
                finish(u - 1)
            softmax(u)
        finish(len(units) - 1)


def _attention(noshift, qt, k, vt):
    batch, _, seq = qt.shape
    group_w = ATTN_GROUP * ATTN_HEAD_DIM
    tq_step = ATTN_Q_TILE * ATTN_Q_TILES_PER_STEP
    grid_spec = pltpu.PrefetchScalarGridSpec(
        num_scalar_prefetch=1,
        grid=(batch, ATTN_KV_HEADS, seq // tq_step),
        in_specs=[
            pl.BlockSpec((1, group_w, tq_step), lambda b, kk, t, *_: (b, kk, t)),
            pl.BlockSpec((1, seq, ATTN_HEAD_DIM), lambda b, kk, t, *_: (b, 0, kk)),
            pl.BlockSpec((1, 1, ATTN_VT_ROWS, seq), lambda b, kk, t, *_: (b, kk, 0, 0)),
        ],
        out_specs=pl.BlockSpec((1, tq_step, group_w), lambda b, kk, t, *_: (b, t, kk)),
        scratch_shapes=[pltpu.VMEM((2, seq, ATTN_Q_TILE), F32), pltpu.VMEM((2, seq, ATTN_Q_TILE), BF16)],
    )
    return pl.pallas_call(
        _attention_kernel,
        grid_spec=grid_spec,
        out_shape=jax.ShapeDtypeStruct((batch, seq, ATTN_Q_W), BF16),
        compiler_params=pltpu.CompilerParams(
            dimension_semantics=("arbitrary", "arbitrary", "arbitrary"),
            vmem_limit_bytes=V7X_VMEM_LIMIT_BYTES),
        name="attention",
    )(noshift, qt, k, vt)


def kernel(x, norm_mix, norm_mlp, mlp_w1, mlp_w2, ret_w_in, ret_w_out, ret_decay_fwd, ret_decay_bwd,
           attn_w_in, attn_w_out, attn_q_norm, attn_k_norm, final_norm):
    batch, seq, d = x.shape
    assert norm_mix.shape[0] == 2 and ret_w_in.shape[0] == 1 and attn_w_in.shape[0] == 1
    assert seq % WIDE_TOKEN_TILE == 0 and seq % TOKEN_TILE == 0 and seq % GRID_W == 0
    tokens = batch * seq
    h = x.reshape(tokens, d)

    cos_r, sin_r = _axial_rope_tables(seq, RET_QK_DIM)
    cos_a, sin_a = _axial_rope_tables(seq, ATTN_HEAD_DIM)
    cos2 = np.concatenate([cos_a, cos_a], axis=-1)
    sin2 = np.concatenate([-sin_a, sin_a], axis=-1)
    w1 = mlp_w1.astype(BF16)
    w2 = mlp_w2.astype(BF16)

    q, kt, v, gs = _ret_proj(h, norm_mix[0][None, :], ret_w_in[0].astype(BF16),
                             (cos_r, sin_r, np.ascontiguousarray(cos_r.T), np.ascontiguousarray(sin_r.T)),
                             batch, seq)
    y = _retention(q.reshape(batch, seq, RET_QK_W), kt, v.reshape(batch, seq, RET_V_W),
                   gs.reshape(batch, seq, RET_V_W), ret_decay_fwd[0], ret_decay_bwd[0])
    h, qta, ka, vta = _tail_attn_proj(
        y.reshape(tokens, RET_V_W), h, ret_w_out[0].astype(BF16), norm_mlp[0][None, :],
        w1, w2, 0, norm_mix[1][None, :],
        attn_w_in[0].astype(BF16),
        jnp.broadcast_to(attn_q_norm[0][:, None], (ATTN_HEAD_DIM, TOKEN_TILE)), attn_k_norm[0][None, :],
        cos2, sin2, np.ascontiguousarray(cos_a.T), np.ascontiguousarray(sin_a.T), batch, seq)

    score_bound = (ATTN_HEAD_DIM * ATTN_Q_SCALE * 1.01) * (
        jnp.max(jnp.abs(attn_q_norm[0])) * jnp.max(jnp.abs(attn_k_norm[0])))
    noshift = (score_bound <= ATTN_NOSHIFT_MAX_LOG2).astype(jnp.int32).reshape(1)
    o = _attention(noshift, qta, ka.reshape(batch, seq, ATTN_KV_W), vta)
    out = _tail_final(o.reshape(tokens, d), h, attn_w_out[0].astype(BF16), norm_mlp[1][None, :],
                      w1, w2, 1, final_norm[None, :])
    return out.reshape(batch, seq, d)
```

```python
import math

import jax
import jax.numpy as jnp
import numpy as np
from jax import lax
from jax.experimental import pallas as pl
from jax.experimental.pallas import tpu as pltpu

BF16 = jnp.bfloat16
F32 = jnp.float32

GRID_W = 64
RMS_EPS = 1e-6
GN_EPS = 1e-6
ROPE_THETA = 10000.0

RET_HEADS = 4
RET_QK_DIM = 256
RET_V_DIM = 512
RET_QK_W = RET_HEADS * RET_QK_DIM
RET_V_W = RET_HEADS * RET_V_DIM
RET_CHUNK = 256
RET_HEADS_PER_STEP = 2

ATTN_HEAD_DIM = 128
ATTN_Q_HEADS = 8
ATTN_KV_HEADS = 2
ATTN_GROUP = ATTN_Q_HEADS // ATTN_KV_HEADS
ATTN_Q_W = ATTN_Q_HEADS * ATTN_HEAD_DIM
ATTN_KV_W = ATTN_KV_HEADS * ATTN_HEAD_DIM

TOKEN_TILE = 512
WIDE_TOKEN_TILE = 1024
ATTN_Q_TILE = 256
ATTN_Q_TILES_PER_STEP = 4
ATTN_KV_CHUNK = 256
ATTN_VT_ROWS = ATTN_HEAD_DIM + 16
ATTN_Q_SCALE = ATTN_HEAD_DIM ** -0.5 * math.log2(math.e)
ATTN_NOSHIFT_MAX_LOG2 = 64.0
FF_CHUNK = 1024
V7X_VMEM_LIMIT_BYTES = 56 * 1024 * 1024


def _dot(a, b):
    return jnp.dot(a, b, preferred_element_type=F32)


def _dot_nt(a, b):
    return lax.dot_general(a, b, (((1,), (1,)), ((), ())), preferred_element_type=F32)


def _rms_scale(x):
    return lax.rsqrt(jnp.mean(x * x, axis=-1, keepdims=True) + RMS_EPS)


def _rms_norm_rows(x, gain):
    return x * _rms_scale(x) * gain


def _resident(shape):
    nd = len(shape)
    return pl.BlockSpec(shape, lambda *_: (0,) * nd, pipeline_mode=pl.Buffered(1))


def _axial_rope_tables(seq, rot_dim):
    rows = seq // GRID_W
    row = np.repeat(np.arange(rows, dtype=np.float64), GRID_W)
    col = np.tile(np.arange(GRID_W, dtype=np.float64), rows)
    per_axis = rot_dim // 2
    n_freq = per_axis // 2
    inv_freq = ROPE_THETA ** (-np.arange(n_freq, dtype=np.float64) * 2.0 / per_axis)
    ang = np.concatenate([row[:, None] * inv_freq[None, :], col[:, None] * inv_freq[None, :]], axis=-1)
    return np.cos(ang).astype(np.float32), np.sin(ang).astype(np.float32)


def _ret_proj_kernel(h_ref, gain_ref, w_ref, cos_ref, sin_ref, cost_ref, sint_ref,
                     q_ref, kt_ref, v_ref, gs_ref, wkt_scr):
    k_col0, v_col0, g_col0 = RET_QK_W, 2 * RET_QK_W, 2 * RET_QK_W + RET_V_W

    @pl.when(pl.program_id(0) == 0)
    def _():
        for hh in range(RET_HEADS):
            lo = hh * RET_QK_DIM
            wkt_scr[lo:lo + RET_QK_DIM, :] = w_ref[:, k_col0 + lo:k_col0 + lo + RET_QK_DIM].T

    x = h_ref[...]
    u = (x * gain_ref[...]).astype(BF16)
    r = _rms_scale(x)
    half = RET_QK_DIM // 2
    r_t = jnp.broadcast_to(r, (x.shape[0], half)).T
    for c in range(RET_HEADS):
        lo = c * RET_V_DIM
        g = _dot(u, w_ref[:, g_col0 + lo:g_col0 + lo + RET_V_DIM]) * r
        gs_ref[:, lo:lo + RET_V_DIM] = (g * jax.nn.sigmoid(g)).astype(BF16)
    k_scale = RET_QK_DIM ** -0.5
    cos_t = cost_ref[...] * (r_t * k_scale)
    sin_t = sint_ref[...] * (r_t * k_scale)
    for hh in range(RET_HEADS):
        lo = hh * RET_QK_DIM
        kh = _dot_nt(wkt_scr[lo:lo + RET_QK_DIM, :], u)
        x1 = kh[:half]
        x2 = kh[half:]
        kt_ref[0, lo:lo + half, :] = (x1 * cos_t - x2 * sin_t).astype(BF16)
        kt_ref[0, lo + half:lo + RET_QK_DIM, :] = (x1 * sin_t + x2 * cos_t).astype(BF16)
    cos = cos_ref[...] * r
    sin = sin_ref[...] * r
    for hh in range(RET_HEADS):
        lo = hh * RET_QK_DIM
        qh = _dot(u, w_ref[:, lo:lo + RET_QK_DIM])
        x1 = qh[:, :half]
        x2 = qh[:, half:]
        q_ref[:, lo:lo + half] = (x1 * cos - x2 * sin).astype(BF16)
        q_ref[:, lo + half:lo + RET_QK_DIM] = (x1 * sin + x2 * cos).astype(BF16)
    for c in range(RET_HEADS):
        lo = c * RET_V_DIM
        v_ref[:, lo:lo + RET_V_DIM] = (_dot(u, w_ref[:, v_col0 + lo:v_col0 + lo + RET_V_DIM]) * r).astype(BF16)


def _ret_proj(h, gain, w, cos_tabs, batch, seq):
    cos, sin, cos_t, sin_t = cos_tabs
    tokens, d = h.shape
    tm = WIDE_TOKEN_TILE
    tiles_per_seq = seq // tm
    half = RET_QK_DIM // 2
    row_tile = lambda width: pl.BlockSpec((tm, width), lambda i: (i, 0))
    return pl.pallas_call(
        _ret_proj_kernel,
        grid=(tokens // tm,),
        in_specs=[
            row_tile(d),
            _resident((1, d)),
            _resident(w.shape),
            pl.BlockSpec((tm, half), lambda i: (i % tiles_per_seq, 0)),
            pl.BlockSpec((tm, half), lambda i: (i % tiles_per_seq, 0)),
            pl.BlockSpec((half, tm), lambda i: (0, i % tiles_per_seq)),
            pl.BlockSpec((half, tm), lambda i: (0, i % tiles_per_seq)),
        ],
        out_specs=[
            row_tile(RET_QK_W),
            pl.BlockSpec((1, RET_QK_W, tm), lambda i: (i // tiles_per_seq, 0, i % tiles_per_seq)),
            row_tile(RET_V_W),
            row_tile(RET_V_W),
        ],
        out_shape=[
            jax.ShapeDtypeStruct((tokens, RET_QK_W), BF16),
            jax.ShapeDtypeStruct((batch, RET_QK_W, seq), BF16),
            jax.ShapeDtypeStruct((tokens, RET_V_W), BF16),
            jax.ShapeDtypeStruct((tokens, RET_V_W), BF16),
        ],
        scratch_shapes=[pltpu.VMEM((RET_QK_W, d), BF16)],
        compiler_params=pltpu.CompilerParams(
            dimension_semantics=("arbitrary",), vmem_limit_bytes=V7X_VMEM_LIMIT_BYTES),
        name="ret_proj",
    )(h, gain, w, cos, sin, cos_t, sin_t)


def _retention_kernel(dec_f_ref, dec_b_ref, q_ref, kt_ref, v_ref, gs_ref, y_ref, sf_scr, sb_scr):
    c = RET_CHUNK
    dk = RET_QK_DIM
    dv = RET_V_DIM
    n_chunks = q_ref.shape[1] // c
    row = lax.broadcasted_iota(jnp.int32, (c, c), 0).astype(F32)
    col = lax.broadcasted_iota(jnp.int32, (c, c), 1).astype(F32)
    diff = row - col

    def rows(i):
        return slice(i * c, (i + 1) * c)

    for hh in range(RET_HEADS_PER_STEP):
        head = pl.program_id(1) * RET_HEADS_PER_STEP + hh
        qk_cols = slice(hh * dk, (hh + 1) * dk)
        v_cols = slice(hh * dv, (hh + 1) * dv)
        lgf = -jnp.exp(jnp.full((c, c), dec_f_ref[head], F32))
        lgb = -jnp.exp(jnp.full((c, c), dec_b_ref[head], F32))
        decay = jnp.where(diff >= 0, jnp.exp(lgf * jnp.maximum(diff, 0.0)),
                          jnp.exp(lgb * jnp.maximum(-diff, 0.0)))
        xi_f = jnp.exp(lgf * (row + 1.0)).astype(BF16)
        xi_b = jnp.exp(lgb * (float(c) - row)).astype(BF16)
        zeta_f = jnp.exp(lgf * (float(c) - 1.0 - col)).astype(BF16)
        zeta_b = jnp.exp(lgb * col).astype(BF16)
        gc_f = jnp.exp(-jnp.exp(jnp.full((dk, dv), dec_f_ref[head], F32)) * float(c))
        gc_b = jnp.exp(-jnp.exp(jnp.full((dk, dv), dec_b_ref[head], F32)) * float(c))

        state = None
        for i in range(n_chunks - 1):
            upd = _dot(kt_ref[0, qk_cols, rows(i)] * zeta_f, v_ref[0, rows(i), v_cols])
            state = upd if state is None else state * gc_f + upd
            sf_scr[hh, i + 1] = state.astype(BF16)
        state = None
        for i in range(n_chunks - 1, 0, -1):
            upd = _dot(kt_ref[0, qk_cols, rows(i)] * zeta_b, v_ref[0, rows(i), v_cols])
            state = upd if state is None else state * gc_b + upd
            sb_scr[hh, i - 1] = state.astype(BF16)

        for i in range(n_chunks):
            qi = q_ref[0, rows(i), qk_cols]
            scores = _dot(qi, kt_ref[0, qk_cols, rows(i)]) * decay
            y = _dot(scores.astype(BF16), v_ref[0, rows(i), v_cols])
            if i > 0:
                y = y + _dot(qi * xi_f, sf_scr[hh, i])
            if i < n_chunks - 1:
                y = y + _dot(qi * xi_b, sb_scr[hh, i])
            mu = jnp.mean(y, axis=-1, keepdims=True)
            yc = y - mu
            var = jnp.mean(yc * yc, axis=-1, keepdims=True)
            yn = (yc * lax.rsqrt(var + GN_EPS)).astype(BF16)
            y_ref[0, rows(i), v_cols] = gs_ref[0, rows(i), v_cols] * yn


def _retention(q, kt, v, gs, dec_f, dec_b):
    batch, seq, _ = q.shape
    assert RET_CHUNK == RET_QK_DIM and seq % RET_CHUNK == 0
    hps = RET_HEADS_PER_STEP
    grid_spec = pltpu.PrefetchScalarGridSpec(
        num_scalar_prefetch=2,
        grid=(batch, RET_HEADS // hps),
        in_specs=[
            pl.BlockSpec((1, seq, hps * RET_QK_DIM), lambda b, h, *_: (b, 0, h)),
            pl.BlockSpec((1, hps * RET_QK_DIM, seq), lambda b, h, *_: (b, h, 0)),
            pl.BlockSpec((1, seq, hps * RET_V_DIM), lambda b, h, *_: (b, 0, h)),
            pl.BlockSpec((1, seq, hps * RET_V_DIM), lambda b, h, *_: (b, 0, h)),
        ],
        out_specs=pl.BlockSpec((1, seq, hps * RET_V_DIM), lambda b, h, *_: (b, 0, h)),
        scratch_shapes=[pltpu.VMEM((hps, seq // RET_CHUNK, RET_QK_DIM, RET_V_DIM), BF16)] * 2,
    )
    return pl.pallas_call(
        _retention_kernel,
        grid_spec=grid_spec,
        out_shape=jax.ShapeDtypeStruct((batch, seq, RET_V_W), BF16),
        compiler_params=pltpu.CompilerParams(
            dimension_semantics=("arbitrary", "arbitrary"), vmem_limit_bytes=V7X_VMEM_LIMIT_BYTES),
        name="retention",
    )(dec_f, dec_b, q, kt, v, gs)


def _mix_out_mlp(y_ref, h_ref, wo_ref, gmlp_ref, w1_ref, w2_ref):
    h1 = h_ref[...] + _dot(y_ref[...], wo_ref[...])
    u = (h1 * gmlp_ref[...]).astype(BF16)
    r = _rms_scale(h1)
    mlp = None
    d_ff = w1_ref.shape[2]
    for c in range(d_ff // FF_CHUNK):
        lo = c * FF_CHUNK
        a = jnp.maximum(_dot(u, w1_ref[0, :, lo:lo + FF_CHUNK]), 0.0)
        part = _dot((a * a).astype(BF16), w2_ref[0, lo:lo + FF_CHUNK, :])
        mlp = part if mlp is None else mlp + part
    return h1 + (r * r) * mlp


def _tail_attn_proj_kernel(y_ref, h_ref, wo_ref, gmlp_ref, w1_ref, w2_ref,
                           gmix_ref, wa_ref, qn_ref, kn_ref,
                           cos2_ref, sin2_ref, cost_ref, sint_ref,
                           hout_ref, qt_ref, k_ref, vt_ref, wqvt_scr):
    k_col0, v_col0 = ATTN_Q_W, ATTN_Q_W + ATTN_KV_W

    @pl.when(pl.program_id(0) == 0)
    def _():
        blk = 2 * ATTN_HEAD_DIM
        wqvt_scr[0:ATTN_KV_W, :] = wa_ref[:, v_col0:v_col0 + ATTN_KV_W].T
        for lo in range(0, ATTN_Q_W, blk):
            wqvt_scr[ATTN_KV_W + lo:ATTN_KV_W + lo + blk, :] = wa_ref[:, lo:lo + blk].T

    h2 = _mix_out_mlp(y_ref, h_ref, wo_ref, gmlp_ref, w1_ref, w2_ref)
    hout_ref[...] = h2
    u = (h2 * gmix_ref[...]).astype(BF16)
    r = _rms_scale(h2)
    d = ATTN_HEAD_DIM
    half = d // 2
    r_t = jnp.broadcast_to(r, (h2.shape[0], d)).T[:1]

    cos2 = cos2_ref[...]
    sin2 = sin2_ref[...]
    k2 = _dot(u, wa_ref[:, k_col0:k_col0 + ATTN_KV_W]) * r
    for hh in range(ATTN_KV_HEADS):
        kn = _rms_norm_rows(k2[:, hh * d:(hh + 1) * d], kn_ref[...])
        k_ref[:, hh * d:(hh + 1) * d] = (kn * cos2 + pltpu.roll(kn, half, 1) * sin2).astype(BF16)
    qv = _dot_nt(wqvt_scr[...], u)
    vt = qv[:ATTN_KV_W]
    for hh in range(ATTN_KV_HEADS):
        vt_ref[0, hh, :d, :] = (vt[hh * d:(hh + 1) * d] * r_t).astype(BF16)
        vt_ref[0, hh, d:, :] = jnp.ones((ATTN_VT_ROWS - d, vt.shape[1]), BF16)

    cos_t = cost_ref[...]
    sin_t = sint_ref[...]
    gain_t = qn_ref[...]
    for pair in range(ATTN_Q_HEADS // 2):
        q2 = qv[ATTN_KV_W + pair * 2 * d:ATTN_KV_W + (pair + 1) * 2 * d]
        for hh in range(2):
            x = q2[hh * d:(hh + 1) * d]
            ms = jnp.mean(x * x, axis=0, keepdims=True) * (r_t * r_t)
            xn = x * (lax.rsqrt(ms + RMS_EPS) * r_t * ATTN_Q_SCALE) * gain_t
            x1 = xn[:half]
            x2 = xn[half:]
            lo = (pair * 2 + hh) * d
            qt_ref[0, lo:lo + half, :] = (x1 * cos_t - x2 * sin_t).astype(BF16)
            qt_ref[0, lo + half:lo + d, :] = (x1 * sin_t + x2 * cos_t).astype(BF16)


def _tail_final_kernel(y_ref, h_ref, wo_ref, gmlp_ref, w1_ref, w2_ref, gfin_ref, out_ref):
    h2 = _mix_out_mlp(y_ref, h_ref, wo_ref, gmlp_ref, w1_ref, w2_ref)
    out_ref[...] = _rms_norm_rows(h2, gfin_ref[...])


def _tail_common_specs(y, h, wo, w1, w2, layer, tm):
    d = h.shape[1]
    layer_slab = lambda w: pl.BlockSpec((1,) + w.shape[1:], lambda *_: (layer, 0, 0),
                                        pipeline_mode=pl.Buffered(1))
    return [
        pl.BlockSpec((tm, y.shape[1]), lambda i: (i, 0)),
        pl.BlockSpec((tm, d), lambda i: (i, 0)),
        _resident(wo.shape),
        _resident((1, d)),
        layer_slab(w1),
        layer_slab(w2),
    ]


def _tail_attn_proj(y, h, wo, gmlp, w1, w2, layer, gmix, wa, qn_t, kn, cos2, sin2, cos_t, sin_t,
                    batch, seq):
    tokens, d = h.shape
    tm = TOKEN_TILE
    tiles_per_seq = seq // tm
    half = ATTN_HEAD_DIM // 2
    pos_tile = pl.BlockSpec((tm, ATTN_HEAD_DIM), lambda i: (i % tiles_per_seq, 0))
    pos_tile_t = pl.BlockSpec((half, tm), lambda i: (0, i % tiles_per_seq))
    return pl.pallas_call(
        _tail_attn_proj_kernel,
        grid=(tokens // tm,),
        in_specs=_tail_common_specs(y, h, wo, w1, w2, layer, tm) + [
            _resident((1, d)),
            _resident(wa.shape),
            _resident(qn_t.shape),
            _resident((1, ATTN_HEAD_DIM)),
            pos_tile,
            pos_tile,
            pos_tile_t,
            pos_tile_t,
        ],
        out_specs=[
            pl.BlockSpec((tm, d), lambda i: (i, 0)),
            pl.BlockSpec((1, ATTN_Q_W, tm), lambda i: (i // tiles_per_seq, 0, i % tiles_per_seq)),
            pl.BlockSpec((tm, ATTN_KV_W), lambda i: (i, 0)),
            pl.BlockSpec((1, ATTN_KV_HEADS, ATTN_VT_ROWS, tm),
                         lambda i: (i // tiles_per_seq, 0, 0, i % tiles_per_seq)),
        ],
        out_shape=[
            jax.ShapeDtypeStruct((tokens, d), F32),
            jax.ShapeDtypeStruct((batch, ATTN_Q_W, seq), BF16),
            jax.ShapeDtypeStruct((tokens, ATTN_KV_W), BF16),
            jax.ShapeDtypeStruct((batch, ATTN_KV_HEADS, ATTN_VT_ROWS, seq), BF16),
        ],
        scratch_shapes=[pltpu.VMEM((ATTN_KV_W + ATTN_Q_W, d), BF16)],
        compiler_params=pltpu.CompilerParams(
            dimension_semantics=("arbitrary",), vmem_limit_bytes=V7X_VMEM_LIMIT_BYTES),
        name="tail_attn_proj",
    )(y, h, wo, gmlp, w1, w2, gmix, wa, qn_t, kn, cos2, sin2, cos_t, sin_t)


def _tail_final(y, h, wo, gmlp, w1, w2, layer, gfin):
    tokens, d = h.shape
    tm = WIDE_TOKEN_TILE
    return pl.pallas_call(
        _tail_final_kernel,
        grid=(tokens // tm,),
        in_specs=_tail_common_specs(y, h, wo, w1, w2, layer, tm) + [_resident((1, d))],
        out_specs=pl.BlockSpec((tm, d), lambda i: (i, 0)),
        out_shape=jax.ShapeDtypeStruct((tokens, d), F32),
        compiler_params=pltpu.CompilerParams(
            dimension_semantics=("arbitrary",), vmem_limit_bytes=V7X_VMEM_LIMIT_BYTES),
        name="tail_final",
    )(y, h, wo, gmlp, w1, w2, gfin)


def _attention_kernel(noshift_ref, qt_ref, k_ref, vt_ref, o_ref, s_scr, p_scr):
    d = ATTN_HEAD_DIM
    seq = k_ref.shape[1]
    tq = ATTN_Q_TILE
    kc = ATTN_KV_CHUNK
    units = [(j, g) for j in range(ATTN_Q_TILES_PER_STEP) for g in range(ATTN_GROUP)]

    def scores(u):
        j, g = units[u]
        qt = qt_ref[0, g * d:(g + 1) * d, j * tq:(j + 1) * tq]
        for c in range(seq // kc):
            s_scr[u % 2, c * kc:(c + 1) * kc, :] = _dot(k_ref[0, c * kc:(c + 1) * kc, :], qt)

    def softmax(u):
        m = None
        for c in range(seq // kc):
            mc = jnp.max(s_scr[u % 2, c * kc:(c + 1) * kc, :], axis=0, keepdims=True)
            m = mc if m is None else jnp.maximum(m, mc)
        for c in range(seq // kc):
            p = jnp.exp2(s_scr[u % 2, c * kc:(c + 1) * kc, :] - m)
            p_scr[u % 2, c * kc:(c + 1) * kc, :] = p.astype(BF16)

    def finish(u):
        j, g = units[u]
        acc = _dot(vt_ref[0, 0], p_scr[u % 2])
        o_ref[0, j * tq:(j + 1) * tq, g * d:(g + 1) * d] = (acc[:d] / acc[d:d + 1]).T.astype(BF16)

    def scores_exp(u):
        j, g = units[u]
        qt = qt_ref[0, g * d:(g + 1) * d, j * tq:(j + 1) * tq]
        l = None
        for c in range(seq // kc):
            p = jnp.exp2(_dot(k_ref[0, c * kc:(c + 1) * kc, :], qt))
            p_scr[u % 2, c * kc:(c + 1) * kc, :] = p.astype(BF16)
            lc = jnp.sum(p, axis=0, keepdims=True)
            l = lc if l is None else l + lc
        return l

    def finish_noshift(u, l):
        j, g = units[u]
        acc = _dot(vt_ref[0, 0, :d, :], p_scr[u % 2])
        o_ref[0, j * tq:(j + 1) * tq, g * d:(g + 1) * d] = (acc / l).T.astype(BF16)

    @pl.when(noshift_ref[0] == 1)
    def _():
        l_next = scores_exp(0)
        for u in range(len(units)):
            l = l_next
            if u + 1 < len(units):
                l_next = scores_exp(u + 1)
            finish_noshift(u, l)

    @pl.when(noshift_ref[0] != 1)
    def _():
        scores(0)
        for u in range(len(units)):
            if u + 1 < len(units):
                scores(u + 1)
            if u > 0:
                finish(u - 1)
            softmax(u)
        finish(len(units) - 1)


def _attention(noshift, qt, k, vt):
    batch, _, seq = qt.shape
    group_w = ATTN_GROUP * ATTN_HEAD_DIM
    tq_step = ATTN_Q_TILE * ATTN_Q_TILES_PER_STEP
    grid_spec = pltpu.PrefetchScalarGridSpec(
        num_scalar_prefetch=1,
        grid=(batch, ATTN_KV_HEADS, seq // tq_step),
        in_specs=[
            pl.BlockSpec((1, group_w, tq_step), lambda b, kk, t, *_: (b, kk, t)),
            pl.BlockSpec((1, seq, ATTN_HEAD_DIM), lambda b, kk, t, *_: (b, 0, kk)),
            pl.BlockSpec((1, 1, ATTN_VT_ROWS, seq), lambda b, kk, t, *_: (b, kk, 0, 0)),
        ],
        out_specs=pl.BlockSpec((1, tq_step, group_w), lambda b, kk, t, *_: (b, t, kk)),
        scratch_shapes=[pltpu.VMEM((2, seq, ATTN_Q_TILE), F32), pltpu.VMEM((2, seq, ATTN_Q_TILE), BF16)],
    )
    return pl.pallas_call(
        _attention_kernel,
        grid_spec=grid_spec,
        out_shape=jax.ShapeDtypeStruct((batch, seq, ATTN_Q_W), BF16),
        compiler_params=pltpu.CompilerParams(
            dimension_semantics=("arbitrary", "arbitrary", "arbitrary"),
            vmem_limit_bytes=V7X_VMEM_LIMIT_BYTES),
        name="attention",
    )(noshift, qt, k, vt)


def kernel(x, norm_mix, norm_mlp, mlp_w1, mlp_w2, ret_w_in, ret_w_out, ret_decay_fwd, ret_decay_bwd,
           attn_w_in, attn_w_out, attn_q_norm, attn_k_norm, final_norm):
    batch, seq, d = x.shape
    assert norm_mix.shape[0] == 2 and ret_w_in.shape[0] == 1 and attn_w_in.shape[0] == 1
    assert seq % WIDE_TOKEN_TILE == 0 and seq % TOKEN_TILE == 0 and seq % GRID_W == 0
    tokens = batch * seq
    h = x.reshape(tokens, d)

    cos_r, sin_r = _axial_rope_tables(seq, RET_QK_DIM)
    cos_a, sin_a = _axial_rope_tables(seq, ATTN_HEAD_DIM)
    cos2 = np.concatenate([cos_a, cos_a], axis=-1)
    sin2 = np.concatenate([-sin_a, sin_a], axis=-1)
    w1 = mlp_w1.astype(BF16)
    w2 = mlp_w2.astype(BF16)

    q, kt, v, gs = _ret_proj(h, norm_mix[0][None, :], ret_w_in[0].astype(BF16),
                             (cos_r, sin_r, np.ascontiguousarray(cos_r.T), np.ascontiguousarray(sin_r.T)),
                             batch, seq)
    y = _retention(q.reshape(batch, seq, RET_QK_W), kt, v.reshape(batch, seq, RET_V_W),
                   gs.reshape(batch, seq, RET_V_W), ret_decay_fwd[0], ret_decay_bwd[0])
    h, qta, ka, vta = _tail_attn_proj(
        y.reshape(tokens, RET_V_W), h, ret_w_out[0].astype(BF16), norm_mlp[0][None, :],
        w1, w2, 0, norm_mix[1][None, :],
        attn_w_in[0].astype(BF16),
        jnp.broadcast_to(attn_q_norm[0][:, None], (ATTN_HEAD_DIM, TOKEN_TILE)), attn_k_norm[0][None, :],
        cos2, sin2, np.ascontiguousarray(cos_a.T), np.ascontiguousarray(sin_a.T), batch, seq)

    score_bound = (ATTN_HEAD_DIM * ATTN_Q_SCALE * 1.01) * (
        jnp.max(jnp.abs(attn_q_norm[0])) * jnp.max(jnp.abs(attn_k_norm[0])))
    noshift = (score_bound <= ATTN_NOSHIFT_MAX_LOG2).astype(jnp.int32).reshape(1)
    o = _attention(noshift, qta, ka.reshape(batch, seq, ATTN_KV_W), vta)
    out = _tail_final(o.reshape(tokens, d), h, attn_w_out[0].astype(BF16), norm_mlp[1][None, :],
                      w1, w2, 1, final_norm[None, :])
    return out.reshape(batch, seq, d)
```

```python
import math

import jax
import jax.numpy as jnp
import numpy as np
from jax import lax
from jax.experimental import pallas as pl
from jax.experimental.pallas import tpu as pltpu

BF16 = jnp.bfloat16
F32 = jnp.float32

GRID_W = 64
RMS_EPS = 1e-6
GN_EPS = 1e-6
ROPE_THETA = 10000.0

RET_HEADS = 4
RET_QK_DIM = 256
RET_V_DIM = 512
RET_QK_W = RET_HEADS * RET_QK_DIM
RET_V_W = RET_HEADS * RET_V_DIM
RET_CHUNK = 256
RET_HEADS_PER_STEP = 2

ATTN_HEAD_DIM = 128
ATTN_Q_HEADS = 8
ATTN_KV_HEADS = 2
ATTN_GROUP = ATTN_Q_HEADS // ATTN_KV_HEADS
ATTN_Q_W = ATTN_Q_HEADS * ATTN_HEAD_DIM
ATTN_KV_W = ATTN_KV_HEADS * ATTN_HEAD_DIM

TOKEN_TILE = 512
WIDE_TOKEN_TILE = 1024
ATTN_Q_TILE = 256
ATTN_Q_TILES_PER_STEP = 4
ATTN_KV_CHUNK = 256
ATTN_VT_ROWS = ATTN_HEAD_DIM + 16
ATTN_Q_SCALE = ATTN_HEAD_DIM ** -0.5 * math.log2(math.e)
ATTN_NOSHIFT_MAX_LOG2 = 64.0
FF_CHUNK = 1024
V7X_VMEM_LIMIT_BYTES = 56 * 1024 * 1024


def _dot(a, b):
    return jnp.dot(a, b, preferred_element_type=F32)


def _dot_nt(a, b):
    return lax.dot_general(a, b, (((1,), (1,)), ((), ())), preferred_element_type=F32)


def _rms_scale(x):
    return lax.rsqrt(jnp.mean(x * x, axis=-1, keepdims=True) + RMS_EPS)


def _rms_norm_rows(x, gain):
    return x * _rms_scale(x) * gain


def _resident(shape):
    nd = len(shape)
    return pl.BlockSpec(shape, lambda *_: (0,) * nd, pipeline_mode=pl.Buffered(1))


def _axial_rope_tables(seq, rot_dim):
    rows = seq // GRID_W
    row = np.repeat(np.arange(rows, dtype=np.float64), GRID_W)
    col = np.tile(np.arange(GRID_W, dtype=np.float64), rows)
    per_axis = rot_dim // 2
    n_freq = per_axis // 2
    inv_freq = ROPE_THETA ** (-np.arange(n_freq, dtype=np.float64) * 2.0 / per_axis)
    ang = np.concatenate([row[:, None] * inv_freq[None, :], col[:, None] * inv_freq[None, :]], axis=-1)
    return np.cos(ang).astype(np.float32), np.sin(ang).astype(np.float32)


def _ret_proj_kernel(h_ref, gain_ref, w_ref, cos_ref, sin_ref, q_ref, kt_ref, v_ref, gs_ref):
    k_col0, v_col0, g_col0 = RET_QK_W, 2 * RET_QK_W, 2 * RET_QK_W + RET_V_W

    x = h_ref[...]
    u = (x * gain_ref[...]).astype(BF16)
    r = _rms_scale(x)
    half = RET_QK_DIM // 2
    for c in range(RET_HEADS):
        lo = c * RET_V_DIM
        g = _dot(u, w_ref[:, g_col0 + lo:g_col0 + lo + RET_V_DIM]) * r
        gs_ref[:, lo:lo + RET_V_DIM] = (g * jax.nn.sigmoid(g)).astype(BF16)
    cos = cos_ref[...] * r
    sin = sin_ref[...] * r
    k_scale = RET_QK_DIM ** -0.5
    cos_k = cos * k_scale
    sin_k = sin * k_scale
    for hh in range(RET_HEADS):
        lo = hh * RET_QK_DIM
        kh = _dot(u, w_ref[:, k_col0 + lo:k_col0 + lo + RET_QK_DIM])
        x1 = kh[:, :half]
        x2 = kh[:, half:]
        kt_ref[0, lo:lo + half, :] = (x1 * cos_k - x2 * sin_k).T.astype(BF16)
        kt_ref[0, lo + half:lo + RET_QK_DIM, :] = (x1 * sin_k + x2 * cos_k).T.astype(BF16)
    for hh in range(RET_HEADS):
        lo = hh * RET_QK_DIM
        qh = _dot(u, w_ref[:, lo:lo + RET_QK_DIM])
        x1 = qh[:, :half]
        x2 = qh[:, half:]
        q_ref[:, lo:lo + half] = (x1 * cos - x2 * sin).astype(BF16)
        q_ref[:, lo + half:lo + RET_QK_DIM] = (x1 * sin + x2 * cos).astype(BF16)
    for c in range(RET_HEADS):
        lo = c * RET_V_DIM
        v_ref[:, lo:lo + RET_V_DIM] = (_dot(u, w_ref[:, v_col0 + lo:v_col0 + lo + RET_V_DIM]) * r).astype(BF16)


def _ret_proj(h, gain, w, cos, sin, batch, seq):
    tokens, d = h.shape
    tm = WIDE_TOKEN_TILE
    tiles_per_seq = seq // tm
    half = RET_QK_DIM // 2
    row_tile = lambda width: pl.BlockSpec((tm, width), lambda i: (i, 0))
    return pl.pallas_call(
        _ret_proj_kernel,
        grid=(tokens // tm,),
        in_specs=[
            row_tile(d),
            _resident((1, d)),
            _resident(w.shape),
            pl.BlockSpec((tm, half), lambda i: (i % tiles_per_seq, 0)),
            pl.BlockSpec((tm, half), lambda i: (i % tiles_per_seq, 0)),
        ],
        out_specs=[
            row_tile(RET_QK_W),
            pl.BlockSpec((1, RET_QK_W, tm), lambda i: (i // tiles_per_seq, 0, i % tiles_per_seq)),
            row_tile(RET_V_W),
            row_tile(RET_V_W),
        ],
        out_shape=[
            jax.ShapeDtypeStruct((tokens, RET_QK_W), BF16),
            jax.ShapeDtypeStruct((batch, RET_QK_W, seq), BF16),
            jax.ShapeDtypeStruct((tokens, RET_V_W), BF16),
            jax.ShapeDtypeStruct((tokens, RET_V_W), BF16),
        ],
        compiler_params=pltpu.CompilerParams(
            dimension_semantics=("arbitrary",), vmem_limit_bytes=V7X_VMEM_LIMIT_BYTES),
        name="ret_proj",
    )(h, gain, w, cos, sin)


def _retention_kernel(dec_f_ref, dec_b_ref, q_ref, kt_ref, v_ref, gs_ref, y_ref, sf_scr, sb_scr):
    c = RET_CHUNK
    dk = RET_QK_DIM
    dv = RET_V_DIM
    n_chunks = q_ref.shape[1] // c
    row = lax.broadcasted_iota(jnp.int32, (c, c), 0).astype(F32)
    col = lax.broadcasted_iota(jnp.int32, (c, c), 1).astype(F32)
    diff = row - col

    def rows(i):
        return slice(i * c, (i + 1) * c)

    for hh in range(RET_HEADS_PER_STEP):
        head = pl.program_id(1) * RET_HEADS_PER_STEP + hh
        qk_cols = slice(hh * dk, (hh + 1) * dk)
        v_cols = slice(hh * dv, (hh + 1) * dv)
        lgf = -jnp.exp(jnp.full((c, c), dec_f_ref[head], F32))
        lgb = -jnp.exp(jnp.full((c, c), dec_b_ref[head], F32))
        decay = jnp.where(diff >= 0, jnp.exp(lgf * jnp.maximum(diff, 0.0)),
                          jnp.exp(lgb * jnp.maximum(-diff, 0.0)))
        xi_f = jnp.exp(lgf * (row + 1.0)).astype(BF16)
        xi_b = jnp.exp(lgb * (float(c) - row)).astype(BF16)
        zeta_f = jnp.exp(lgf * (float(c) - 1.0 - col)).astype(BF16)
        zeta_b = jnp.exp(lgb * col).astype(BF16)
        gc_f = jnp.exp(-jnp.exp(jnp.full((dk, dv), dec_f_ref[head], F32)) * float(c))
        gc_b = jnp.exp(-jnp.exp(jnp.full((dk, dv), dec_b_ref[head], F32)) * float(c))

        state = None
        for i in range(n_chunks - 1):
            upd = _dot(kt_ref[0, qk_cols, rows(i)] * zeta_f, v_ref[0, rows(i), v_cols])
            state = upd if state is None else state * gc_f + upd
            sf_scr[hh, i + 1] = state.astype(BF16)
        state = None
        for i in range(n_chunks - 1, 0, -1):
            upd = _dot(kt_ref[0, qk_cols, rows(i)] * zeta_b, v_ref[0, rows(i), v_cols])
            state = upd if state is None else state * gc_b + upd
            sb_scr[hh, i - 1] = state.astype(BF16)

        for i in range(n_chunks):
            qi = q_ref[0, rows(i), qk_cols]
            scores = _dot(qi, kt_ref[0, qk_cols, rows(i)]) * decay
            y = _dot(scores.astype(BF16), v_ref[0, rows(i), v_cols])
            if i > 0:
                y = y + _dot(qi * xi_f, sf_scr[hh, i])
            if i < n_chunks - 1:
                y = y + _dot(qi * xi_b, sb_scr[hh, i])
            mu = jnp.mean(y, axis=-1, keepdims=True)
            yc = y - mu
            var = jnp.mean(yc * yc, axis=-1, keepdims=True)
            yn = (yc * lax.rsqrt(var + GN_EPS)).astype(BF16)
            y_ref[0, rows(i), v_cols] = gs_ref[0, rows(i), v_cols] * yn


def _retention(q, kt, v, gs, dec_f, dec_b):
    batch, seq, _ = q.shape
    assert RET_CHUNK == RET_QK_DIM and seq % RET_CHUNK == 0
    hps = RET_HEADS_PER_STEP
    grid_spec = pltpu.PrefetchScalarGridSpec(
        num_scalar_prefetch=2,
        grid=(batch, RET_HEADS // hps),
        in_specs=[
            pl.BlockSpec((1, seq, hps * RET_QK_DIM), lambda b, h, *_: (b, 0, h)),
            pl.BlockSpec((1, hps * RET_QK_DIM, seq), lambda b, h, *_: (b, h, 0)),
            pl.BlockSpec((1, seq, hps * RET_V_DIM), lambda b, h, *_: (b, 0, h)),
            pl.BlockSpec((1, seq, hps * RET_V_DIM), lambda b, h, *_: (b, 0, h)),
        ],
        out_specs=pl.BlockSpec((1, seq, hps * RET_V_DIM), lambda b, h, *_: (b, 0, h)),
        scratch_shapes=[pltpu.VMEM((hps, seq // RET_CHUNK, RET_QK_DIM, RET_V_DIM), BF16)] * 2,
    )
    return pl.pallas_call(
        _retention_kernel,
        grid_spec=grid_spec,
        out_shape=jax.ShapeDtypeStruct((batch, seq, RET_V_W), BF16),
        compiler_params=pltpu.CompilerParams(
            dimension_semantics=("arbitrary", "arbitrary"), vmem_limit_bytes=V7X_VMEM_LIMIT_BYTES),
        name="retention",
    )(dec_f, dec_b, q, kt, v, gs)


def _mix_out_mlp(y_ref, h_ref, wo_ref, gmlp_ref, w1_ref, w2_ref):
    h1 = h_ref[...] + _dot(y_ref[...], wo_ref[...])
    u = (h1 * gmlp_ref[...]).astype(BF16)
    r = _rms_scale(h1)
    mlp = None
    d_ff = w1_ref.shape[2]
    for c in range(d_ff // FF_CHUNK):
        lo = c * FF_CHUNK
        a = jnp.maximum(_dot(u, w1_ref[0, :, lo:lo + FF_CHUNK]), 0.0)
        part = _dot((a * a).astype(BF16), w2_ref[0, lo:lo + FF_CHUNK, :])
        mlp = part if mlp is None else mlp + part
    return h1 + (r * r) * mlp


def _tail_attn_proj_kernel(y_ref, h_ref, wo_ref, gmlp_ref, w1_ref, w2_ref,
                           gmix_ref, wa_ref, qn_ref, kn_ref,
                           cos2_ref, sin2_ref, cost_ref, sint_ref,
                           hout_ref, qt_ref, k_ref, vt_ref, wqvt_scr):
    k_col0, v_col0 = ATTN_Q_W, ATTN_Q_W + ATTN_KV_W

    @pl.when(pl.program_id(0) == 0)
    def _():
        blk = 2 * ATTN_HEAD_DIM
        wqvt_scr[0:ATTN_KV_W, :] = wa_ref[:, v_col0:v_col0 + ATTN_KV_W].T
        for lo in range(0, ATTN_Q_W, blk):
            wqvt_scr[ATTN_KV_W + lo:ATTN_KV_W + lo + blk, :] = wa_ref[:, lo:lo + blk].T

    h2 = _mix_out_mlp(y_ref, h_ref, wo_ref, gmlp_ref, w1_ref, w2_ref)
    hout_ref[...] = h2
    u = (h2 * gmix_ref[...]).astype(BF16)
    r = _rms_scale(h2)
    d = ATTN_HEAD_DIM
    half = d // 2
    r_t = jnp.broadcast_to(r, (h2.shape[0], d)).T[:1]

    cos2 = cos2_ref[...]
    sin2 = sin2_ref[...]
    k2 = _dot(u, wa_ref[:, k_col0:k_col0 + ATTN_KV_W]) * r
    for hh in range(ATTN_KV_HEADS):
        kn = _rms_norm_rows(k2[:, hh * d:(hh + 1) * d], kn_ref[...])
        k_ref[:, hh * d:(hh + 1) * d] = (kn * cos2 + pltpu.roll(kn, half, 1) * sin2).astype(BF16)
    qv = _dot_nt(wqvt_scr[...], u)
    vt = qv[:ATTN_KV_W]
    for hh in range(ATTN_KV_HEADS):
        vt_ref[0, hh, :d, :] = (vt[hh * d:(hh + 1) * d] * r_t).astype(BF16)
        vt_ref[0, hh, d:, :] = jnp.ones((ATTN_VT_ROWS - d, vt.shape[1]), BF16)

    cos_t = cost_ref[...]
    sin_t = sint_ref[...]
    gain_t = qn_ref[...]
    for pair in range(ATTN_Q_HEADS // 2):
        q2 = qv[ATTN_KV_W + pair * 2 * d:ATTN_KV_W + (pair + 1) * 2 * d]
        for hh in range(2):
            x = q2[hh * d:(hh + 1) * d]
            ms = jnp.mean(x * x, axis=0, keepdims=True) * (r_t * r_t)
            xn = x * (lax.rsqrt(ms + RMS_EPS) * r_t * ATTN_Q_SCALE) * gain_t
            x1 = xn[:half]
            x2 = xn[half:]
            lo = (pair * 2 + hh) * d
            qt_ref[0, lo:lo + half, :] = (x1 * cos_t - x2 * sin_t).astype(BF16)
            qt_ref[0, lo + half:lo + d, :] = (x1 * sin_t + x2 * cos_t).astype(BF16)


def _tail_final_kernel(y_ref, h_ref, wo_ref, gmlp_ref, w1_ref, w2_ref, gfin_ref, out_ref):
    h2 = _mix_out_mlp(y_ref, h_ref, wo_ref, gmlp_ref, w1_ref, w2_ref)
    out_ref[...] = _rms_norm_rows(h2, gfin_ref[...])


def _tail_common_specs(y, h, wo, w1, w2, layer, tm):
    d = h.shape[1]
    layer_slab = lambda w: pl.BlockSpec((1,) + w.shape[1:], lambda *_: (layer, 0, 0),
                                        pipeline_mode=pl.Buffered(1))
    return [
        pl.BlockSpec((tm, y.shape[1]), lambda i: (i, 0)),
        pl.BlockSpec((tm, d), lambda i: (i, 0)),
        _resident(wo.shape),
        _resident((1, d)),
        layer_slab(w1),
        layer_slab(w2),
    ]


def _tail_attn_proj(y, h, wo, gmlp, w1, w2, layer, gmix, wa, qn_t, kn, cos2, sin2, cos_t, sin_t,
                    batch, seq):
    tokens, d = h.shape
    tm = TOKEN_TILE
    tiles_per_seq = seq // tm
    half = ATTN_HEAD_DIM // 2
    pos_tile = pl.BlockSpec((tm, ATTN_HEAD_DIM), lambda i: (i % tiles_per_seq, 0))
    pos_tile_t = pl.BlockSpec((half, tm), lambda i: (0, i % tiles_per_seq))
    return pl.pallas_call(
        _tail_attn_proj_kernel,
        grid=(tokens // tm,),
        in_specs=_tail_common_specs(y, h, wo, w1, w2, layer, tm) + [
            _resident((1, d)),
            _resident(wa.shape),
            _resident(qn_t.shape),
            _resident((1, ATTN_HEAD_DIM)),
            pos_tile,
            pos_tile,
            pos_tile_t,
            pos_tile_t,
        ],
        out_specs=[
            pl.BlockSpec((tm, d), lambda i: (i, 0)),
            pl.BlockSpec((1, ATTN_Q_W, tm), lambda i: (i // tiles_per_seq, 0, i % tiles_per_seq)),
            pl.BlockSpec((tm, ATTN_KV_W), lambda i: (i, 0)),
            pl.BlockSpec((1, ATTN_KV_HEADS, ATTN_VT_ROWS, tm),
                         lambda i: (i // tiles_per_seq, 0, 0, i % tiles_per_seq)),
        ],
        out_shape=[
            jax.ShapeDtypeStruct((tokens, d), F32),
            jax.ShapeDtypeStruct((batch, ATTN_Q_W, seq), BF16),
            jax.ShapeDtypeStruct((tokens, ATTN_KV_W), BF16),
            jax.ShapeDtypeStruct((batch, ATTN_KV_HEADS, ATTN_VT_ROWS, seq), BF16),
        ],
        scratch_shapes=[pltpu.VMEM((ATTN_KV_W + ATTN_Q_W, d), BF16)],
        compiler_params=pltpu.CompilerParams(
            dimension_semantics=("arbitrary",), vmem_limit_bytes=V7X_VMEM_LIMIT_BYTES),
        name="tail_attn_proj",
    )(y, h, wo, gmlp, w1, w2, gmix, wa, qn_t, kn, cos2, sin2, cos_t, sin_t)


def _tail_final(y, h, wo, gmlp, w1, w2, layer, gfin):
    tokens, d = h.shape
    tm = WIDE_TOKEN_TILE
    return pl.pallas_call(
        _tail_final_kernel,
        grid=(tokens // tm,),
        in_specs=_tail_common_specs(y, h, wo, w1, w2, layer, tm) + [_resident((1, d))],
        out_specs=pl.BlockSpec((tm, d), lambda i: (i, 0)),
        out_shape=jax.ShapeDtypeStruct((tokens, d), F32),
        compiler_params=pltpu.CompilerParams(
            dimension_semantics=("arbitrary",), vmem_limit_bytes=V7X_VMEM_LIMIT_BYTES),
        name="tail_final",
    )(y, h, wo, gmlp, w1, w2, gfin)


def _attention_kernel(noshift_ref, qt_ref, k_ref, vt_ref, o_ref, s_scr, p_scr):
    d = ATTN_HEAD_DIM
    seq = k_ref.shape[1]
    tq = ATTN_Q_TILE
    kc = ATTN_KV_CHUNK
    units = [(j, g) for j in range(ATTN_Q_TILES_PER_STEP) for g in range(ATTN_GROUP)]

    def scores(u):
        j, g = units[u]
        qt = qt_ref[0, g * d:(g + 1) * d, j * tq:(j + 1) * tq]
        for c in range(seq // kc):
            s_scr[u % 2, c * kc:(c + 1) * kc, :] = _dot(k_ref[0, c * kc:(c + 1) * kc, :], qt)

    def softmax(u):
        m = None
        for c in range(seq // kc):
            mc = jnp.max(s_scr[u % 2, c * kc:(c + 1) * kc, :], axis=0, keepdims=True)
            m = mc if m is None else jnp.maximum(m, mc)
        for c in range(seq // kc):
            p = jnp.exp2(s_scr[u % 2, c * kc:(c + 1) * kc, :] - m)
            p_scr[u % 2, c * kc:(c + 1) * kc, :] = p.astype(BF16)

    def finish(u):
        j, g = units[u]
        acc = _dot(vt_ref[0, 0], p_scr[u % 2])
        o_ref[0, j * tq:(j + 1) * tq, g * d:(g + 1) * d] = (acc[:d] / acc[d:d + 1]).T.astype(BF16)

    def scores_exp(u):
        j, g = units[u]
        qt = qt_ref[0, g * d:(g + 1) * d, j * tq:(j + 1) * tq]
        l = None
        for c in range(seq // kc):
            p = jnp.exp2(_dot(k_ref[0, c * kc:(c + 1) * kc, :], qt))
            p_scr[u % 2, c * kc:(c + 1) * kc, :] = p.astype(BF16)
            lc = jnp.sum(p, axis=0, keepdims=True)
            l = lc if l is None else l + lc
        return l

    def finish_noshift(u, l):
        j, g = units[u]
        acc = _dot(vt_ref[0, 0, :d, :], p_scr[u % 2])
        o_ref[0, j * tq:(j + 1) * tq, g * d:(g + 1) * d] = (acc / l).T.astype(BF16)

    @pl.when(noshift_ref[0] == 1)
    def _():
        l_next = scores_exp(0)
        for u in range(len(units)):
            l = l_next
            if u + 1 < len(units):
                l_next = scores_exp(u + 1)
            finish_noshift(u, l)

    @pl.when(noshift_ref[0] != 1)
    def _():
        scores(0)
        for u in range(len(units)):
            if u + 1 < len(units):
                scores(u + 1)
            if u > 0:
                finish(u - 1)
            softmax(u)
        finish(len(units) - 1)


def _attention(noshift, qt, k, vt):
    batch, _, seq = qt.shape
    group_w = ATTN_GROUP * ATTN_HEAD_DIM
    tq_step = ATTN_Q_TILE * ATTN_Q_TILES_PER_STEP
    grid_spec = pltpu.PrefetchScalarGridSpec(
        num_scalar_prefetch=1,
        grid=(batch, ATTN_KV_HEADS, seq // tq_step),
        in_specs=[
            pl.BlockSpec((1, group_w, tq_step), lambda b, kk, t, *_: (b, kk, t)),
            pl.BlockSpec((1, seq, ATTN_HEAD_DIM), lambda b, kk, t, *_: (b, 0, kk)),
            pl.BlockSpec((1, 1, ATTN_VT_ROWS, seq), lambda b, kk, t, *_: (b, kk, 0, 0)),
        ],
        out_specs=pl.BlockSpec((1, tq_step, group_w), lambda b, kk, t, *_: (b, t, kk)),
        scratch_shapes=[pltpu.VMEM((2, seq, ATTN_Q_TILE), F32), pltpu.VMEM((2, seq, ATTN_Q_TILE), BF16)],
    )
    return pl.pallas_call(
        _attention_kernel,
        grid_spec=grid_spec,
        out_shape=jax.ShapeDtypeStruct((batch, seq, ATTN_Q_W), BF16),
        compiler_params=pltpu.CompilerParams(
            dimension_semantics=("arbitrary", "arbitrary", "arbitrary"),
            vmem_limit_bytes=V7X_VMEM_LIMIT_BYTES),
        name="attention",
    )(noshift, qt, k, vt)


def kernel(x, norm_mix, norm_mlp, mlp_w1, mlp_w2, ret_w_in, ret_w_out, ret_decay_fwd, ret_decay_bwd,
           attn_w_in, attn_w_out, attn_q_norm, attn_k_norm, final_norm):
    batch, seq, d = x.shape
    assert norm_mix.shape[0] == 2 and ret_w_in.shape[0] == 1 and attn_w_in.shape[0] == 1
    assert seq % WIDE_TOKEN_TILE == 0 and seq % TOKEN_TILE == 0 and seq % GRID_W == 0
    tokens = batch * seq
    h = x.reshape(tokens, d)

    cos_r, sin_r = _axial_rope_tables(seq, RET_QK_DIM)
    cos_a, sin_a = _axial_rope_tables(seq, ATTN_HEAD_DIM)
    cos2 = np.concatenate([cos_a, cos_a], axis=-1)
    sin2 = np.concatenate([-sin_a, sin_a], axis=-1)
    w1 = mlp_w1.astype(BF16)
    w2 = mlp_w2.astype(BF16)

    q, kt, v, gs = _ret_proj(h, norm_mix[0][None, :], ret_w_in[0].astype(BF16), cos_r, sin_r, batch, seq)
    y = _retention(q.reshape(batch, seq, RET_QK_W), kt, v.reshape(batch, seq, RET_V_W),
                   gs.reshape(batch, seq, RET_V_W), ret_decay_fwd[0], ret_decay_bwd[0])
    h, qta, ka, vta = _tail_attn_proj(
        y.reshape(tokens, RET_V_W), h, ret_w_out[0].astype(BF16), norm_mlp[0][None, :],
        w1, w2, 0, norm_mix[1][None, :],
        attn_w_in[0].astype(BF16),
        jnp.broadcast_to(attn_q_norm[0][:, None], (ATTN_HEAD_DIM, TOKEN_TILE)), attn_k_norm[0][None, :],
        cos2, sin2, np.ascontiguousarray(cos_a.T), np.ascontiguousarray(sin_a.T), batch, seq)

    score_bound = (ATTN_HEAD_DIM * ATTN_Q_SCALE * 1.01) * (
        jnp.max(jnp.abs(attn_q_norm[0])) * jnp.max(jnp.abs(attn_k_norm[0])))
    noshift = (score_bound <= ATTN_NOSHIFT_MAX_LOG2).astype(jnp.int32).reshape(1)
    o = _attention(noshift, qta, ka.reshape(batch, seq, ATTN_KV_W), vta)
    out = _tail_final(o.reshape(tokens, d), h, attn_w_out[0].astype(BF16), norm_mlp[1][None, :],
                      w1, w2, 1, final_norm[None, :])
    return out.reshape(batch, seq, d)
```
